```python
import jax, jax.numpy as jnp
from jax import lax
import numpy as np

D_MODEL = 2048
BATCH = 4
SEQ = 2048
DEPTH = 2
DEC_BATCH = 8
DEC_SEQ = 4
PAST_LEN = 16384
PAGE_SIZE = 128

N_EVEN = (DEPTH + 1) // 2
N_ODD = DEPTH // 2
MIX_WIDTH = D_MODEL
A_WIDTH = MIX_WIDTH // 2
CHUNK = 128
A_GROUPS = 8
A_GROUP_DIM = A_WIDTH // A_GROUPS
SB_HEAD_DIM = 128
SB_HEADS = (MIX_WIDTH // 2) // SB_HEAD_DIM
SB_WIDTH = SB_HEADS * SB_HEAD_DIM
SB_SCALE = SB_HEAD_DIM ** -0.5
SB_BIAS_INIT = -8.0
Q_BLOCK = 128
POOL_WINDOWS = (2, 4, 8, 16)
C_GROUPS = 4
C_WIDTH = MIX_WIDTH // 2
C_GROUP_DIM = C_WIDTH // C_GROUPS
POOL_BUF = max(POOL_WINDOWS) - 1
D_WIDTH = MIX_WIDTH // 2
CONV_WIDTH = 3
FFN_DIM = 5632
EPS = 1e-6
EVEN_IN = 3 * SB_WIDTH + 2 * A_WIDTH
ODD_IN = C_WIDTH + 3 * D_WIDTH

kernel_name = 'hybrid_gmlp_stickbreak_pool_shortconv_step'


def rms_norm(x, g):
    xf = x.astype(jnp.float32)
    y = xf * lax.rsqrt(jnp.mean(xf * xf, axis=-1, keepdims=True) + EPS)
    return (y * g.astype(jnp.float32)).astype(x.dtype)


def swiglu(h, w_gate, w_up, w_down):
    return (jax.nn.silu(h @ w_gate) * (h @ w_up)) @ w_down


def sb_weights(z, mask):
    log_keep = jnp.where(mask, jax.nn.log_sigmoid(-z), 0.0)
    later = lax.cumsum(log_keep, axis=z.ndim - 1, reverse=True) - log_keep
    return jnp.where(mask, jnp.exp(jax.nn.log_sigmoid(z) + later), 0.0)


def sb_prompt(q, k, v, bias):
    b, T, H, Dh = q.shape
    nb = T // Q_BLOCK
    k_pos = jnp.arange(T)
    qb = q.reshape(b, nb, Q_BLOCK, H, Dh).swapaxes(0, 1)
    bias_f = bias.astype(jnp.float32)[None, :, None, None]

    def block(args):
        qi, blk = args
        q_pos = blk * Q_BLOCK + jnp.arange(Q_BLOCK)
        z = jnp.einsum('bqhd,bkhd->bhqk', qi, k).astype(jnp.float32) * SB_SCALE + bias_f
        a = sb_weights(z, k_pos[None, :] < q_pos[:, None])
        return jnp.einsum('bhqk,bkhd->bqhd', a.astype(v.dtype), v)

    out = lax.map(block, (qb, jnp.arange(nb)))
    return out.swapaxes(0, 1).reshape(b, T, H * Dh)


def sb_sample(q, k_new, v_new, k_past, v_past, bias):
    b, Tq, H, Dh = q.shape
    P = k_past.shape[1]
    z = jnp.concatenate([jnp.einsum('bqhd,bkhd->bhqk', q, k_past),
                         jnp.einsum('bqhd,bkhd->bhqk', q, k_new)], axis=-1).astype(jnp.float32) * SB_SCALE
    z = z + bias.astype(jnp.float32)[None, :, None, None]
    q_pos = P + jnp.arange(Tq)
    k_pos = jnp.arange(P + Tq)
    a = sb_weights(z, k_pos[None, :] < q_pos[:, None]).astype(v_new.dtype)
    out = (jnp.einsum('bhqk,bkhd->bqhd', a[..., :P], v_past)
           + jnp.einsum('bhqk,bkhd->bqhd', a[..., P:], v_new))
    return out.reshape(b, Tq, H * Dh)


def chunk_gating(au, av, ws, bs, rows):
    b, T, _ = av.shape
    nc = T // rows
    causal = jnp.tril(jnp.ones((rows, rows), dtype=bool))
    w = jnp.where(causal, ws[:, :rows, :rows], 0.0).astype(av.dtype)
    v = av.reshape(b, nc, rows, A_GROUPS, A_GROUP_DIM)
    mix = jnp.einsum('gpq,bcqgd->bcpgd', w, v) + bs[:, :rows].T[:, :, None]
    return au * mix.reshape(b, T, A_WIDTH)


def even_project(h, w_in, q_g, k_g, av_g):
    b, T, _ = h.shape
    p = h @ w_in
    q, k, v, au, av = jnp.split(p, [SB_WIDTH, 2 * SB_WIDTH, 3 * SB_WIDTH, 3 * SB_WIDTH + A_WIDTH], axis=-1)
    q = rms_norm(q.reshape(b, T, SB_HEADS, SB_HEAD_DIM), q_g)
    k = rms_norm(k.reshape(b, T, SB_HEADS, SB_HEAD_DIM), k_g)
    v = v.reshape(b, T, SB_HEADS, SB_HEAD_DIM)
    au = jax.nn.gelu(au)
    av = rms_norm(jax.nn.gelu(av), av_g)
    return q, k, v, au, av


def pool_mix(xc, prev, pos0, c_wg, c_scale):
    full = xc if prev is None else jnp.concatenate([prev, xc], axis=1)
    b, T, _ = xc.shape
    L = full.shape[1] - T
    ff = full.astype(jnp.float32).reshape(b, L + T, C_GROUPS, C_GROUP_DIM)
    cs = jnp.concatenate([jnp.zeros_like(ff[:, :1]), jnp.cumsum(ff, axis=1)], axis=1)
    win = jnp.array(POOL_WINDOWS, dtype=jnp.int32)
    hi = L + jnp.arange(T) + 1
    lo = jnp.maximum(hi[:, None] - win[None, :], 0)
    s = cs[:, hi] - cs[:, lo, jnp.arange(C_GROUPS)]
    count = jnp.minimum(pos0 + jnp.arange(T)[:, None] + 1, win[None, :]).astype(jnp.float32)
    pooled = (s / count[..., None] - ff[:, L:]).astype(xc.dtype)
    mixed = jnp.einsum('btgi,gio->btgo', pooled, c_wg).reshape(b, T, C_WIDTH) * c_scale
    return mixed, full[:, -POOL_BUF:]


def short_conv(g, prev, w):
    T = g.shape[1]
    full = jnp.concatenate([prev, g], axis=1)
    out = full[:, 0:T] * w[0]
    for tap in range(1, CONV_WIDTH):
        out = out + full[:, tap:tap + T] * w[tap]
    return out, full[:, -(CONV_WIDTH - 1):]


def odd_mixer(h, w_in, w_out, c_wg, c_scale, d_conv_w, pool_prev, conv_prev, pos0):
    p = h @ w_in
    xc, gb, gc, gh = jnp.split(p, [C_WIDTH, C_WIDTH + D_WIDTH, C_WIDTH + 2 * D_WIDTH], axis=-1)
    c_out, pool_state = pool_mix(xc, pool_prev, pos0, c_wg, c_scale)
    conv_out, conv_state = short_conv(gc * gh, conv_prev, d_conv_w)
    d_out = gb * conv_out
    return jnp.concatenate([c_out, d_out], axis=-1) @ w_out, pool_state, conv_state


def setup_inputs(seed: int = 0) -> dict:
    key = jax.random.key(seed)
    ks = jax.random.split(key, 24)
    f32 = jnp.float32
    n_pages = PAST_LEN // PAGE_SIZE
    n_used = DEC_BATCH * n_pages
    n_phys = n_used + max(1, n_used // 4)

    def dense(k, shape, fan_in):
        return jax.random.normal(k, shape, f32) * (fan_in ** -0.5)

    def gain(k, shape):
        return 1.0 + 0.05 * jax.random.normal(k, shape, f32)

    page_table = jax.random.permutation(ks[6], n_phys)[:n_used].reshape(DEC_BATCH, n_pages).astype(jnp.int32)
    return {
        'x_prompt': jax.random.normal(ks[0], (BATCH, SEQ, D_MODEL), f32),
        'x_sample': jax.random.normal(ks[1], (DEC_BATCH, DEC_SEQ, D_MODEL), f32),
        'cache_k': jax.random.normal(ks[2], (N_EVEN, n_phys, PAGE_SIZE, SB_HEADS, SB_HEAD_DIM), f32),
        'cache_v': jax.random.normal(ks[3], (N_EVEN, n_phys, PAGE_SIZE, SB_HEADS, SB_HEAD_DIM), f32),
        'state_pool': jax.random.normal(ks[4], (N_ODD, DEC_BATCH, POOL_BUF, C_WIDTH), f32),
        'state_conv': jax.random.normal(ks[5], (N_ODD, DEC_BATCH, CONV_WIDTH - 1, D_WIDTH), f32),
        'page_table': page_table,
        'norm_g': gain(ks[7], (DEPTH, 3, D_MODEL)),
        'ffn_w_gate': dense(ks[8], (DEPTH, 2, D_MODEL, FFN_DIM), D_MODEL),
        'ffn_w_up': dense(ks[9], (DEPTH, 2, D_MODEL, FFN_DIM), D_MODEL),
        'ffn_w_down': dense(ks[10], (DEPTH, 2, FFN_DIM, D_MODEL), FFN_DIM),
        'w_in_even': dense(ks[11], (N_EVEN, D_MODEL, EVEN_IN), D_MODEL),
        'w_out_even': dense(ks[12], (N_EVEN, A_WIDTH + SB_WIDTH, D_MODEL), A_WIDTH + SB_WIDTH),
        'q_norm_g': gain(ks[13], (N_EVEN, SB_HEAD_DIM)),
        'k_norm_g': gain(ks[14], (N_EVEN, SB_HEAD_DIM)),
        'sb_bias': SB_BIAS_INIT + 0.1 * jax.random.normal(ks[23], (N_EVEN, SB_HEADS), f32),
        'a_v_norm_g': gain(ks[15], (N_EVEN, A_WIDTH)),
        'a_ws': dense(ks[16], (N_EVEN, A_GROUPS, CHUNK, CHUNK), CHUNK),
        'a_bs': 1.0 + 0.02 * jax.random.normal(ks[17], (N_EVEN, A_GROUPS, CHUNK), f32),
        'w_in_odd': dense(ks[18], (N_ODD, D_MODEL, ODD_IN), D_MODEL),
        'w_out_odd': dense(ks[19], (N_ODD, C_WIDTH + D_WIDTH, D_MODEL), C_WIDTH + D_WIDTH),
        'c_wg': dense(ks[20], (N_ODD, C_GROUPS, C_GROUP_DIM, C_GROUP_DIM), C_GROUP_DIM),
        'c_scale': gain(ks[21], (N_ODD, C_WIDTH)),
        'd_conv_w': dense(ks[22], (N_ODD, CONV_WIDTH, D_WIDTH), CONV_WIDTH),
    }


def reference(x_prompt, x_sample, cache_k, cache_v, state_pool, state_conv, page_table,
              norm_g, ffn_w_gate, ffn_w_up, ffn_w_down,
              w_in_even, w_out_even, q_norm_g, k_norm_g, sb_bias, a_v_norm_g, a_ws, a_bs,
              w_in_odd, w_out_odd, c_wg, c_scale, d_conv_w):
    yp, ys = x_prompt, x_sample
    n_pages = page_table.shape[1]
    kp_l, vp_l, ks_l, vs_l, av_l = [], [], [], [], []
    pp_l, ps_l, cp_l, cs_l = [], [], [], []
    for layer in range(DEPTH):
        i = layer // 2
        yp = yp + 0.5 * swiglu(rms_norm(yp, norm_g[layer, 0]), ffn_w_gate[layer, 0], ffn_w_up[layer, 0], ffn_w_down[layer, 0])
        ys = ys + 0.5 * swiglu(rms_norm(ys, norm_g[layer, 0]), ffn_w_gate[layer, 0], ffn_w_up[layer, 0], ffn_w_down[layer, 0])
        hp = rms_norm(yp, norm_g[layer, 1])
        hs = rms_norm(ys, norm_g[layer, 1])
        if layer % 2 == 0:
            q, k, v, au, av = even_project(hp, w_in_even[i], q_norm_g[i], k_norm_g[i], a_v_norm_g[i])
            mix = jnp.concatenate([chunk_gating(au, av, a_ws[i], a_bs[i], CHUNK), sb_prompt(q, k, v, sb_bias[i])], axis=-1)
            yp = yp + mix @ w_out_even[i]
            kp_l.append(k)
            vp_l.append(v)
            q, k, v, au, av = even_project(hs, w_in_even[i], q_norm_g[i], k_norm_g[i], a_v_norm_g[i])
            k_past = cache_k[i][page_table].reshape(DEC_BATCH, n_pages * PAGE_SIZE, SB_HEADS, SB_HEAD_DIM)
            v_past = cache_v[i][page_table].reshape(DEC_BATCH, n_pages * PAGE_SIZE, SB_HEADS, SB_HEAD_DIM)
            mix = jnp.concatenate([chunk_gating(au, av, a_ws[i], a_bs[i], hs.shape[1]),
                                   sb_sample(q, k, v, k_past, v_past, sb_bias[i])], axis=-1)
            ys = ys + mix @ w_out_even[i]
            ks_l.append(k)
            vs_l.append(v)
            av_l.append(av)
        else:
            conv0 = jnp.zeros((hp.shape[0], CONV_WIDTH - 1, D_WIDTH), hp.dtype)
            m, pst, cst = odd_mixer(hp, w_in_odd[i], w_out_odd[i], c_wg[i], c_scale[i], d_conv_w[i], None, conv0, 0)
            yp = yp + m
            pp_l.append(pst)
            cp_l.append(cst)
            m, pst, cst = odd_mixer(hs, w_in_odd[i], w_out_odd[i], c_wg[i], c_scale[i], d_conv_w[i],
                                    state_pool[i], state_conv[i], PAST_LEN)
            ys = ys + m
            ps_l.append(pst)
            cs_l.append(cst)
        yp = yp + 0.5 * swiglu(rms_norm(yp, norm_g[layer, 2]), ffn_w_gate[layer, 1], ffn_w_up[layer, 1], ffn_w_down[layer, 1])
        ys = ys + 0.5 * swiglu(rms_norm(ys, norm_g[layer, 2]), ffn_w_gate[layer, 1], ffn_w_up[layer, 1], ffn_w_down[layer, 1])
    new_k_prompt = jnp.stack(kp_l)
    new_v_prompt = jnp.stack(vp_l)
    new_k_sample = jnp.stack(ks_l)
    new_v_sample = jnp.stack(vs_l)
    new_a_v_sample = jnp.stack(av_l)
    new_pool_prompt = jnp.stack(pp_l)
    new_pool_sample = jnp.stack(ps_l)
    new_conv_prompt = jnp.stack(cp_l)
    new_conv_sample = jnp.stack(cs_l)
    return (yp, ys, new_k_prompt, new_v_prompt, new_k_sample, new_v_sample, new_a_v_sample,
            new_pool_prompt, new_pool_sample, new_conv_prompt, new_conv_sample)
```

```python
import functools

import jax
import jax.numpy as jnp
from jax import lax
from jax.experimental import pallas as pl
from jax.experimental.pallas import tpu as pltpu

F32 = jnp.float32
BF16 = jnp.bfloat16

EPS = 1e-6
HEAD_DIM = 128
N_HEADS = 8
SB_SCALE = HEAD_DIM ** -0.5
PAGE = 128
POOL_WINDOWS = (2, 4, 8, 16)
POOL_GROUP = 256
POOL_HIST = 15
CONV_HIST = 2
HIST_PAD = 16
VMEM_LIMIT = 56 * 1024 * 1024


def _params(sem):
    return pltpu.CompilerParams(dimension_semantics=sem, vmem_limit_bytes=VMEM_LIMIT)


def _rms(x, g):
    return x * lax.rsqrt(jnp.mean(x * x, axis=-1, keepdims=True) + EPS) * g


def _dot(a, b):
    return jnp.dot(a, b, preferred_element_type=F32)


def _split_dot(x, u):
    hi = x.astype(BF16)
    lo = (x - hi.astype(F32)).astype(BF16)
    return _dot(hi, u) + _dot(lo, u)


def _ffn_kernel(x_ref, g_ref, wg_ref, wu_ref, wd_ref, o_ref, h_ref, acc_ref):
    j = pl.program_id(1)

    @pl.when(j == 0)
    def _():
        h_ref[...] = _rms(x_ref[...], g_ref[...]).astype(BF16)
        acc_ref[...] = jnp.zeros_like(acc_ref)

    h = h_ref[...]
    gate = _dot(h, wg_ref[...])
    up = _dot(h, wu_ref[...])
    act = (gate * jax.nn.sigmoid(gate) * up).astype(BF16)
    acc_ref[...] += _dot(act, wd_ref[...])

    @pl.when(j == pl.num_programs(1) - 1)
    def _():
        o_ref[...] = x_ref[...] + 0.5 * acc_ref[...]


def _ffn(x, g, wg, wu, wd, *, tm, tf):
    m, d = x.shape
    f = wg.shape[1]
    return pl.pallas_call(
        _ffn_kernel,
        grid=(m // tm, f // tf),
        in_specs=[
            pl.BlockSpec((tm, d), lambda i, j: (i, 0)),
            pl.BlockSpec((1, d), lambda i, j: (0, 0)),
            pl.BlockSpec((d, tf), lambda i, j: (0, j)),
            pl.BlockSpec((d, tf), lambda i, j: (0, j)),
            pl.BlockSpec((tf, d), lambda i, j: (j, 0)),
        ],
        out_specs=pl.BlockSpec((tm, d), lambda i, j: (i, 0)),
        out_shape=jax.ShapeDtypeStruct((m, d), F32),
        scratch_shapes=[pltpu.VMEM((tm, d), BF16), pltpu.VMEM((tm, d), F32)],
        compiler_params=_params(("parallel", "arbitrary")),
        name="ffn",
    )(x, g.reshape(1, d), wg, wu, wd)


def _norm_matmul_kernel(x_ref, g_ref, w_ref, o_ref, h_ref):
    @pl.when(pl.program_id(1) == 0)
    def _():
        h_ref[...] = _rms(x_ref[...], g_ref[...]).astype(BF16)

    o_ref[...] = _dot(h_ref[...], w_ref[...])


def _norm_matmul(x, g, w, *, tm, tn):
    m, d = x.shape
    n = w.shape[1]
    return pl.pallas_call(
        _norm_matmul_kernel,
        grid=(m // tm, n // tn),
        in_specs=[
            pl.BlockSpec((tm, d), lambda i, j: (i, 0)),
            pl.BlockSpec((1, d), lambda i, j: (0, 0)),
            pl.BlockSpec((d, tn), lambda i, j: (0, j)),
        ],
        out_specs=pl.BlockSpec((tm, tn), lambda i, j: (i, j)),
        out_shape=jax.ShapeDtypeStruct((m, n), F32),
        scratch_shapes=[pltpu.VMEM((tm, d), BF16)],
        compiler_params=_params(("parallel", "arbitrary")),
        name="norm_matmul",
    )(x, g.reshape(1, d), w)


def _head_norm(p, g):
    cols = []
    for h in range(N_HEADS):
        cols.append(_rms(p[:, h * HEAD_DIM:(h + 1) * HEAD_DIM], g))
    return jnp.concatenate(cols, axis=-1)


def _even_in_kernel(x_ref, g_ref, w_ref, qg_ref, kg_ref, avg_ref, wmix_ref, bmix_ref,
                    q_ref, k_ref, v_ref, gated_ref, avn_ref, h_ref, au_ref, *, rows):
    j = pl.program_id(1)

    @pl.when(j == 0)
    def _():
        h_ref[...] = _rms(x_ref[...], g_ref[...]).astype(BF16)

    p = _dot(h_ref[...], w_ref[...])

    @pl.when(j == 0)
    def _():
        q_ref[...] = _head_norm(p, qg_ref[...]).astype(BF16)

    @pl.when(j == 1)
    def _():
        k_ref[...] = _head_norm(p, kg_ref[...])

    @pl.when(j == 2)
    def _():
        v_ref[...] = p

    @pl.when(j == 3)
    def _():
        au_ref[...] = jax.nn.gelu(p)

    @pl.when(j == 4)
    def _():
        avn = _rms(jax.nn.gelu(p), avg_ref[...])
        avn_ref[...] = avn
        avb = avn.astype(BF16)
        k_pad = wmix_ref.shape[2] - rows
        for c in range(p.shape[0] // rows):
            rs = slice(c * rows, (c + 1) * rows)
            for grp in range(N_HEADS):
                cs = slice(grp * HEAD_DIM, (grp + 1) * HEAD_DIM)
                blk = avb[rs, cs]
                if k_pad:
                    blk = jnp.concatenate([blk, jnp.zeros((k_pad, HEAD_DIM), BF16)], axis=0)
                mix = _dot(wmix_ref[grp], blk) + bmix_ref[:, cs]
                gated_ref[rs, cs] = (au_ref[rs, cs] * mix).astype(BF16)


def _even_in(x, g, w, qg, kg, avg, wmix, bmix, *, tm):
    m, d = x.shape
    rows = wmix.shape[1]
    assert tm % rows == 0
    sec = N_HEADS * HEAD_DIM
    row_blk = lambda i, j: (i, 0)
    const = lambda i, j: (0, 0)
    out_blk = pl.BlockSpec((tm, sec), row_blk)
    return pl.pallas_call(
        functools.partial(_even_in_kernel, rows=rows),
        grid=(m // tm, 5),
        in_specs=[
            pl.BlockSpec((tm, d), row_blk),
            pl.BlockSpec((1, d), const),
            pl.BlockSpec((d, sec), lambda i, j: (0, j)),
            pl.BlockSpec((1, HEAD_DIM), const),
            pl.BlockSpec((1, HEAD_DIM), const),
            pl.BlockSpec((1, sec), const),
            pl.BlockSpec(wmix.shape, lambda i, j: (0, 0, 0)),
            pl.BlockSpec((rows, sec), const),
        ],
        out_specs=[out_blk] * 5,
        out_shape=[
            jax.ShapeDtypeStruct((m, sec), BF16),
            jax.ShapeDtypeStruct((m, sec), F32),
            jax.ShapeDtypeStruct((m, sec), F32),
            jax.ShapeDtypeStruct((m, sec), BF16),
            jax.ShapeDtypeStruct((m, sec), F32),
        ],
        scratch_shapes=[pltpu.VMEM((tm, d), BF16), pltpu.VMEM((tm, sec), F32)],
        compiler_params=_params(("parallel", "arbitrary")),
        name="even_in",
    )(x, g.reshape(1, d), w, qg.reshape(1, HEAD_DIM), kg.reshape(1, HEAD_DIM),
      avg.reshape(1, sec), wmix, bmix)


def _log_keep(z):
    return -(jnp.maximum(z, 0.0) + jnp.log1p(jnp.exp(-jnp.abs(z))))


def _sb_prompt_kernel(bias_ref, q_ref, k_ref, v_ref, o_ref, kb_ref, vb_ref, *, tq, tk):
    h = pl.program_id(1)
    i = pl.program_id(2)

    @pl.when(i == 0)
    def _():
        kb_ref[...] = k_ref[...].astype(BF16)
        vb_ref[...] = v_ref[...].astype(BF16)

    bias = bias_ref[h]
    q = q_ref[...]
    q_pos = i * tq + lax.broadcasted_iota(jnp.int32, (tq, tk), 0)
    k_off = lax.broadcasted_iota(jnp.int32, (tq, tk), 1)
    upper = (lax.broadcasted_iota(jnp.int32, (tk, tk), 0)
             > lax.broadcasted_iota(jnp.int32, (tk, tk), 1)).astype(BF16)
    n_blk = (i + 1) * (tq // tk)

    def body(it, carry):
        acc, run = carry
        start = pl.multiple_of((n_blk - 1 - it) * tk, tk)
        kb = kb_ref[pl.ds(start, tk), :]
        vb = vb_ref[pl.ds(start, tk), :]
        z = lax.dot_general(q, kb, (((1,), (1,)), ((), ())), preferred_element_type=F32)
        z = z * SB_SCALE + bias
        mask = (start + k_off) < q_pos
        keep = _log_keep(z)
        lk = jnp.where(mask, keep, 0.0)
        later = _split_dot(lk, upper) + run
        a = jnp.where(mask, jnp.exp(z + keep + later), 0.0)
        acc = acc + _dot(a.astype(BF16), vb)
        run = run + jnp.sum(lk, axis=-1, keepdims=True)
        return acc, run

    acc, _ = lax.fori_loop(
        0, n_blk, body, (jnp.zeros((tq, HEAD_DIM), F32), jnp.zeros((tq, 1), F32)))
    o_ref[...] = acc.astype(BF16)


def _sb_prompt(q, k, v, bias, *, batch, tq, tk):
    m, width = q.shape
    t = m // batch
    nq = t // tq
    qo_spec = pl.BlockSpec((tq, HEAD_DIM), lambda b, h, i: (b * nq + i, h))
    kv_spec = pl.BlockSpec((t, HEAD_DIM), lambda b, h, i: (b, h))
    return pl.pallas_call(
        functools.partial(_sb_prompt_kernel, tq=tq, tk=tk),
        grid=(batch, width // HEAD_DIM, nq),
        in_specs=[pl.BlockSpec(memory_space=pltpu.SMEM), qo_spec, kv_spec, kv_spec],
        out_specs=qo_spec,
        out_shape=jax.ShapeDtypeStruct((m, width), BF16),
        scratch_shapes=[pltpu.VMEM((t, HEAD_DIM), BF16), pltpu.VMEM((t, HEAD_DIM), BF16)],
        compiler_params=_params(("parallel", "parallel", "arbitrary")),
        name="sb_prompt",
    )(bias, q, k, v)


def _sb_sample_kernel(pt_ref, qbd_ref, bias_ref, knew_ref, vnew_ref, kpage_ref, vpage_ref,
                      o_ref, acc_ref, run_ref, *, tq):
    j = pl.program_id(1)
    lanes = qbd_ref.shape[-1]
    upper = (lax.broadcasted_iota(jnp.int32, (PAGE, PAGE), 1)
             > lax.broadcasted_iota(jnp.int32, (PAGE, PAGE), 0)).astype(BF16)

    def step(kblk, vblk, mask):
        z = _dot(kblk.astype(BF16), qbd_ref[0]) * SB_SCALE + bias_ref[...]
        keep = _log_keep(z)
        lk = keep if mask is None else jnp.where(mask, keep, 0.0)
        hi = lk.astype(BF16)
        lo = (lk - hi.astype(F32)).astype(BF16)
        later = _dot(upper, hi) + _dot(upper, lo) + run_ref[...]
        a = jnp.exp(z + keep + later)
        if mask is not None:
            a = jnp.where(mask, a, 0.0)
        acc_ref[...] += lax.dot_general(a.astype(BF16), vblk.astype(BF16),
                                        (((0,), (0,)), ((), ())), preferred_element_type=F32)
        run_ref[...] += jnp.sum(lk, axis=0, keepdims=True)

    @pl.when(j == 0)
    def _():
        acc_ref[...] = jnp.zeros_like(acc_ref)
        run_ref[...] = jnp.zeros_like(run_ref)
        s_pos = lax.broadcasted_iota(jnp.int32, (PAGE, lanes), 0)
        t_pos = lax.rem(lax.broadcasted_iota(jnp.int32, (PAGE, lanes), 1), tq)
        step(knew_ref[0], vnew_ref[0], s_pos < t_pos)

    @pl.when(j > 0)
    def _():
        step(kpage_ref[0], vpage_ref[0], None)

    @pl.when(j == pl.num_programs(1) - 1)
    def _():
        for h in range(N_HEADS):
            cs = slice(h * HEAD_DIM, (h + 1) * HEAD_DIM)
            o_ref[0, :, cs] = acc_ref[h * tq:(h + 1) * tq, cs].astype(BF16)


def _sb_sample(q, k_new, v_new, cache_k, cache_v, page_table, bias, *, batch):
    m, width = q.shape
    tq = m // batch
    n_pages = page_table.shape[1]
    used = N_HEADS * tq
    lanes = -(-used // HEAD_DIM) * HEAD_DIM
    q4 = q.reshape(batch, tq, N_HEADS, HEAD_DIM)
    eye = jnp.eye(N_HEADS, dtype=BF16)
    qbd = (q4.transpose(0, 2, 3, 1)[:, :, :, None, :] * eye[None, :, None, :, None]).reshape(
        batch, width, used)
    qbd = jnp.pad(qbd, ((0, 0), (0, 0), (0, lanes - used)))
    bias_l = jnp.pad(jnp.repeat(bias, tq), (0, lanes - used)).reshape(1, lanes)
    pad = ((0, 0), (0, PAGE - tq), (0, 0))
    k_pad = jnp.pad(k_new.reshape(batch, tq, width), pad)
    v_pad = jnp.pad(v_new.reshape(batch, tq, width), pad)

    def page_map(b, j, pt):
        return (pt[b, n_pages - jnp.maximum(j, 1)], 0, 0)

    new_spec = pl.BlockSpec((1, PAGE, width), lambda b, j, pt: (b, 0, 0))
    page_spec = pl.BlockSpec((1, PAGE, width), page_map)
    out = pl.pallas_call(
        functools.partial(_sb_sample_kernel, tq=tq),
        grid_spec=pltpu.PrefetchScalarGridSpec(
            num_scalar_prefetch=1,
            grid=(batch, n_pages + 1),
            in_specs=[
                pl.BlockSpec((1, width, lanes), lambda b, j, pt: (b, 0, 0)),
                pl.BlockSpec((1, lanes), lambda b, j, pt: (0, 0)),
                new_spec, new_spec, page_spec, page_spec,
            ],
            out_specs=pl.BlockSpec((1, tq, width), lambda b, j, pt: (b, 0, 0)),
            scratch_shapes=[pltpu.VMEM((lanes, width), F32), pltpu.VMEM((1, lanes), F32)],
        ),
        out_shape=jax.ShapeDtypeStruct((batch, tq, width), BF16),
        compiler_params=_params(("parallel", "arbitrary")),
        name="sb_sample",
    )(page_table, qbd, bias_l, k_pad, v_pad, cache_k, cache_v)
    return out.reshape(m, width)


def _out_proj_kernel(a_ref, b_ref, wa_ref, wb_ref, r_ref, o_ref):
    o_ref[...] = r_ref[...] + _dot(a_ref[...], wa_ref[...]) + _dot(b_ref[...], wb_ref[...])


def _out_proj(a, b, w, resid, *, tm, tn):
    m, ka = a.shape
    n = w.shape[1]
    assert b.shape[1] == ka and w.shape[0] == 2 * ka
    return pl.pallas_call(
        _out_proj_kernel,
        grid=(m // tm, n // tn),
        in_specs=[
            pl.BlockSpec((tm, ka), lambda i, j: (i, 0)),
            pl.BlockSpec((tm, ka), lambda i, j: (i, 0)),
            pl.BlockSpec((ka, tn), lambda i, j: (0, j)),
            pl.BlockSpec((ka, tn), lambda i, j: (1, j)),
            pl.BlockSpec((tm, tn), lambda i, j: (i, j)),
        ],
        out_specs=pl.BlockSpec((tm, tn), lambda i, j: (i, j)),
        out_shape=jax.ShapeDtypeStruct((m, n), F32),
        compiler_params=_params(("parallel", "parallel")),
        name="out_proj",
    )(a, b, w, w, resid)


def _odd_mix_kernel(p_ref, pool_prev_ref, conv_prev_ref, cwg_ref, cscale_ref, convw_ref,
                    c_ref, d_ref, pool_out_ref, conv_out_ref, xbuf_ref, ubuf_ref, *, tt, pos0):
    t_idx = pl.program_id(1)
    width = c_ref.shape[-1]

    @pl.when(t_idx == 0)
    def _():
        xbuf_ref[HIST_PAD - POOL_HIST:HIST_PAD, :] = pool_prev_ref[0]
        ubuf_ref[HIST_PAD - CONV_HIST:HIST_PAD, :] = conv_prev_ref[0]

    xc = p_ref[0, :, 0:width]
    gb = p_ref[0, :, width:2 * width]
    u = p_ref[0, :, 2 * width:3 * width] * p_ref[0, :, 3 * width:4 * width]
    xbuf_ref[HIST_PAD:HIST_PAD + tt, :] = xc
    ubuf_ref[HIST_PAD:HIST_PAD + tt, :] = u

    seen = pos0 + t_idx * tt + lax.broadcasted_iota(jnp.int32, (tt, 1), 0) + 1
    for grp, win in enumerate(POOL_WINDOWS):
        cs = slice(grp * POOL_GROUP, (grp + 1) * POOL_GROUP)
        s = xc[:, cs]
        for back in range(1, win):
            s = s + xbuf_ref[HIST_PAD - back:HIST_PAD - back + tt, cs]
        count = jnp.minimum(seen, win).astype(F32)
        pooled = s / count - xc[:, cs]
        mixed = _dot(pooled.astype(BF16), cwg_ref[grp]) * cscale_ref[:, cs]
        c_ref[0, :, cs] = mixed.astype(BF16)

    conv = (ubuf_ref[HIST_PAD - 2:HIST_PAD - 2 + tt, :] * convw_ref[0:1, :]
            + ubuf_ref[HIST_PAD - 1:HIST_PAD - 1 + tt, :] * convw_ref[1:2, :]
            + u * convw_ref[2:3, :])
    d_ref[0] = (gb * conv).astype(BF16)

    pool_tail = xbuf_ref[HIST_PAD + tt - POOL_HIST:HIST_PAD + tt, :]
    conv_tail = ubuf_ref[HIST_PAD + tt - CONV_HIST:HIST_PAD + tt, :]
    pool_out_ref[0] = pool_tail
    conv_out_ref[0] = conv_tail
    xbuf_ref[HIST_PAD - POOL_HIST:HIST_PAD, :] = pool_tail
    ubuf_ref[HIST_PAD - CONV_HIST:HIST_PAD, :] = conv_tail


def _odd_mix(p, pool_prev, conv_prev, cwg, cscale, convw, *, tt, pos0):
    batch, t, _ = p.shape
    width = pool_prev.shape[-1]
    seq_blk = lambda b, i: (b, i, 0)
    per_b = lambda b, i: (b, 0, 0)
    return pl.pallas_call(
        functools.partial(_odd_mix_kernel, tt=tt, pos0=pos0),
        grid=(batch, t // tt),
        in_specs=[
            pl.BlockSpec((1, tt, 4 * width), seq_blk),
            pl.BlockSpec((1, POOL_HIST, width), per_b),
            pl.BlockSpec((1, CONV_HIST, width), per_b),
            pl.BlockSpec(cwg.shape, lambda b, i: (0, 0, 0)),
            pl.BlockSpec((1, width), lambda b, i: (0, 0)),
            pl.BlockSpec(convw.shape, lambda b, i: (0, 0)),
        ],
        out_specs=[
            pl.BlockSpec((1, tt, width), seq_blk),
            pl.BlockSpec((1, tt, width), seq_blk),
            pl.BlockSpec((1, POOL_HIST, width), per_b),
            pl.BlockSpec((1, CONV_HIST, width), per_b),
        ],
        out_shape=[
            jax.ShapeDtypeStruct((batch, t, width), BF16),
            jax.ShapeDtypeStruct((batch, t, width), BF16),
            jax.ShapeDtypeStruct((batch, POOL_HIST, width), F32),
            jax.ShapeDtypeStruct((batch, CONV_HIST, width), F32),
        ],
        scratch_shapes=[pltpu.VMEM((HIST_PAD + tt, width), F32),
                        pltpu.VMEM((HIST_PAD + tt, width), F32)],
        compiler_params=_params(("parallel", "arbitrary")),
        name="odd_mix",
    )(p, pool_prev, conv_prev, cwg, cscale.reshape(1, width), convw)


def _pick_tile(m, target):
    return target if m % target == 0 else m


def kernel(x_prompt, x_sample, cache_k, cache_v, state_pool, state_conv, page_table, norm_g, ffn_w_gate, ffn_w_up, ffn_w_down, w_in_even, w_out_even, q_norm_g, k_norm_g, sb_bias, a_v_norm_g, a_ws, a_bs, w_in_odd, w_out_odd, c_wg, c_scale, d_conv_w):
    bp, tp, d = x_prompt.shape
    bs, ts, _ = x_sample.shape
    depth = norm_g.shape[0]
    mp, ms = bp * tp, bs * ts
    sec = N_HEADS * HEAD_DIM
    chunk = a_ws.shape[-1]

    wg_b, wu_b, wd_b = (w.astype(BF16) for w in (ffn_w_gate, ffn_w_up, ffn_w_down))
    w_in_even_b, w_out_even_b = w_in_even.astype(BF16), w_out_even.astype(BF16)
    w_in_odd_b, w_out_odd_b = w_in_odd.astype(BF16), w_out_odd.astype(BF16)
    cwg_b = c_wg.astype(BF16)

    tm_p = _pick_tile(mp, 512)
    tf = _pick_tile(ffn_w_gate.shape[-1], 512)

    def ffn(y, layer, half, tm):
        return _ffn(y, norm_g[layer, 2 * half], wg_b[layer, half], wu_b[layer, half],
                    wd_b[layer, half], tm=tm, tf=tf)

    yp = x_prompt.reshape(mp, d)
    ys = x_sample.reshape(ms, d)
    outs = {name: [] for name in ("kp", "vp", "ks", "vs", "av", "pp", "ps", "cp", "cs")}

    for layer in range(depth):
        i = layer // 2
        yp = ffn(yp, layer, 0, tm_p)
        ys = ffn(ys, layer, 0, ms)
        if layer % 2 == 0:
            tril_p = jnp.tril(a_ws[i]).astype(BF16)
            bias_p = jnp.repeat(a_bs[i].T, HEAD_DIM, axis=1)
            tril_s = jnp.tril(a_ws[i][:, :ts, :ts])
            eye = jnp.eye(bs, dtype=F32)
            wmix_s = (eye[None, :, None, :, None] * tril_s[:, None, :, None, :]).reshape(
                N_HEADS, ms, ms).astype(BF16)
            wmix_s = jnp.pad(wmix_s, ((0, 0), (0, 0), (0, max(0, HEAD_DIM - ms))))
            bias_s = jnp.tile(jnp.repeat(a_bs[i][:, :ts].T, HEAD_DIM, axis=1), (bs, 1))

            q, k, v, gated, _ = _even_in(yp, norm_g[layer, 1], w_in_even_b[i], q_norm_g[i],
                                         k_norm_g[i], a_v_norm_g[i], tril_p, bias_p, tm=tm_p)
            sb = _sb_prompt(q, k, v, sb_bias[i], batch=bp, tq=256, tk=128)
            yp = _out_proj(gated, sb, w_out_even_b[i], yp, tm=tm_p, tn=512)
            outs["kp"].append(k.reshape(bp, tp, N_HEADS, HEAD_DIM))
            outs["vp"].append(v.reshape(bp, tp, N_HEADS, HEAD_DIM))

            q, k, v, gated, avn = _even_in(ys, norm_g[layer, 1], w_in_even_b[i], q_norm_g[i],
                                           k_norm_g[i], a_v_norm_g[i], wmix_s, bias_s, tm=ms)
            n_phys = cache_k.shape[1]
            sb = _sb_sample(q, k, v, cache_k[i].reshape(n_phys, PAGE, sec),
                            cache_v[i].reshape(n_phys, PAGE, sec), page_table, sb_bias[i],
                            batch=bs)
            ys = _out_proj(gated, sb, w_out_even_b[i], ys, tm=ms, tn=512)
            outs["ks"].append(k.reshape(bs, ts, N_HEADS, HEAD_DIM))
            outs["vs"].append(v.reshape(bs, ts, N_HEADS, HEAD_DIM))
            outs["av"].append(avn.reshape(bs, ts, sec))
        else:
            width = c_scale.shape[-1]
            p = _norm_matmul(yp, norm_g[layer, 1], w_in_odd_b[i], tm=tm_p, tn=1024)
            c_out, d_out, pool_st, conv_st = _odd_mix(
                p.reshape(bp, tp, 4 * width),
                jnp.zeros((bp, POOL_HIST, width), F32), jnp.zeros((bp, CONV_HIST, width), F32),
                cwg_b[i], c_scale[i], d_conv_w[i], tt=_pick_tile(tp, 512), pos0=0)
            yp = _out_proj(c_out.reshape(mp, width), d_out.reshape(mp, width), w_out_odd_b[i],
                           yp, tm=tm_p, tn=512)
            outs["pp"].append(pool_st)
            outs["cp"].append(conv_st)

            p = _norm_matmul(ys, norm_g[layer, 1], w_in_odd_b[i], tm=ms, tn=1024)
            past = page_table.shape[1] * PAGE
            c_out, d_out, pool_st, conv_st = _odd_mix(
                p.reshape(bs, ts, 4 * width), state_pool[i], state_conv[i],
                cwg_b[i], c_scale[i], d_conv_w[i], tt=ts, pos0=past)
            ys = _out_proj(c_out.reshape(ms, width), d_out.reshape(ms, width), w_out_odd_b[i],
                           ys, tm=ms, tn=512)
            outs["ps"].append(pool_st)
            outs["cs"].append(conv_st)
        yp = ffn(yp, layer, 1, tm_p)
        ys = ffn(ys, layer, 1, ms)

    return (yp.reshape(bp, tp, d), ys.reshape(bs, ts, d),
            jnp.stack(outs["kp"]), jnp.stack(outs["vp"]),
            jnp.stack(outs["ks"]), jnp.stack(outs["vs"]), jnp.stack(outs["av"]),
            jnp.stack(outs["pp"]), jnp.stack(outs["ps"]),
            jnp.stack(outs["cp"]), jnp.stack(outs["cs"]))
```

```python
import functools

import jax
import jax.numpy as jnp
from jax import lax
from jax.experimental import pallas as pl
from jax.experimental.pallas import tpu as pltpu

F32 = jnp.float32
BF16 = jnp.bfloat16

EPS = 1e-6
HEAD_DIM = 128
N_HEADS = 8
SB_SCALE = HEAD_DIM ** -0.5
PAGE = 128
POOL_WINDOWS = (2, 4, 8, 16)
POOL_GROUP = 256
POOL_HIST = 15
CONV_HIST = 2
HIST_PAD = 16
VMEM_LIMIT = 56 * 1024 * 1024


def _params(sem):
    return pltpu.CompilerParams(dimension_semantics=sem, vmem_limit_bytes=VMEM_LIMIT)


def _rms(x, g):
    return x * lax.rsqrt(jnp.mean(x * x, axis=-1, keepdims=True) + EPS) * g


def _dot(a, b):
    return jnp.dot(a, b, preferred_element_type=F32)


def _ffn_kernel(x_ref, g_ref, wg_ref, wu_ref, wd_ref, o_ref, h_ref, acc_ref):
    j = pl.program_id(1)

    @pl.when(j == 0)
    def _():
        h_ref[...] = _rms(x_ref[...], g_ref[...]).astype(BF16)
        acc_ref[...] = jnp.zeros_like(acc_ref)

    h = h_ref[...]
    gate = _dot(h, wg_ref[...])
    up = _dot(h, wu_ref[...])
    act = (gate * jax.nn.sigmoid(gate) * up).astype(BF16)
    acc_ref[...] += _dot(act, wd_ref[...])

    @pl.when(j == pl.num_programs(1) - 1)
    def _():
        o_ref[...] = x_ref[...] + 0.5 * acc_ref[...]


def _ffn(x, g, wg, wu, wd, *, tm, tf):
    m, d = x.shape
    f = wg.shape[1]
    return pl.pallas_call(
        _ffn_kernel,
        grid=(m // tm, f // tf),
        in_specs=[
            pl.BlockSpec((tm, d), lambda i, j: (i, 0)),
            pl.BlockSpec((1, d), lambda i, j: (0, 0)),
            pl.BlockSpec((d, tf), lambda i, j: (0, j)),
            pl.BlockSpec((d, tf), lambda i, j: (0, j)),
            pl.BlockSpec((tf, d), lambda i, j: (j, 0)),
        ],
        out_specs=pl.BlockSpec((tm, d), lambda i, j: (i, 0)),
        out_shape=jax.ShapeDtypeStruct((m, d), F32),
        scratch_shapes=[pltpu.VMEM((tm, d), BF16), pltpu.VMEM((tm, d), F32)],
        compiler_params=_params(("parallel", "arbitrary")),
        name="ffn",
    )(x, g.reshape(1, d), wg, wu, wd)


def _norm_matmul_kernel(x_ref, g_ref, w_ref, o_ref, h_ref):
    @pl.when(pl.program_id(1) == 0)
    def _():
        h_ref[...] = _rms(x_ref[...], g_ref[...]).astype(BF16)

    o_ref[...] = _dot(h_ref[...], w_ref[...])


def _norm_matmul(x, g, w, *, tm, tn):
    m, d = x.shape
    n = w.shape[1]
    return pl.pallas_call(
        _norm_matmul_kernel,
        grid=(m // tm, n // tn),
        in_specs=[
            pl.BlockSpec((tm, d), lambda i, j: (i, 0)),
            pl.BlockSpec((1, d), lambda i, j: (0, 0)),
            pl.BlockSpec((d, tn), lambda i, j: (0, j)),
        ],
        out_specs=pl.BlockSpec((tm, tn), lambda i, j: (i, j)),
        out_shape=jax.ShapeDtypeStruct((m, n), F32),
        scratch_shapes=[pltpu.VMEM((tm, d), BF16)],
        compiler_params=_params(("parallel", "arbitrary")),
        name="norm_matmul",
    )(x, g.reshape(1, d), w)


def _head_norm(p, g):
    cols = []
    for h in range(N_HEADS):
        cols.append(_rms(p[:, h * HEAD_DIM:(h + 1) * HEAD_DIM], g))
    return jnp.concatenate(cols, axis=-1)


def _even_in_kernel(x_ref, g_ref, w_ref, qg_ref, kg_ref, avg_ref, wmix_ref, bmix_ref,
                    q_ref, k_ref, v_ref, gated_ref, avn_ref, h_ref, au_ref, *, rows):
    j = pl.program_id(1)

    @pl.when(j == 0)
    def _():
        h_ref[...] = _rms(x_ref[...], g_ref[...]).astype(BF16)

    p = _dot(h_ref[...], w_ref[...])

    @pl.when(j == 0)
    def _():
        q_ref[...] = _head_norm(p, qg_ref[...]).astype(BF16)

    @pl.when(j == 1)
    def _():
        k_ref[...] = _head_norm(p, kg_ref[...])

    @pl.when(j == 2)
    def _():
        v_ref[...] = p

    @pl.when(j == 3)
    def _():
        au_ref[...] = jax.nn.gelu(p)

    @pl.when(j == 4)
    def _():
        avn = _rms(jax.nn.gelu(p), avg_ref[...])
        avn_ref[...] = avn
        avb = avn.astype(BF16)
        k_pad = wmix_ref.shape[2] - rows
        for c in range(p.shape[0] // rows):
            rs = slice(c * rows, (c + 1) * rows)
            for grp in range(N_HEADS):
                cs = slice(grp * HEAD_DIM, (grp + 1) * HEAD_DIM)
                blk = avb[rs, cs]
                if k_pad:
                    blk = jnp.concatenate([blk, jnp.zeros((k_pad, HEAD_DIM), BF16)], axis=0)
                mix = _dot(wmix_ref[grp], blk) + bmix_ref[:, cs]
                gated_ref[rs, cs] = (au_ref[rs, cs] * mix).astype(BF16)


def _even_in(x, g, w, qg, kg, avg, wmix, bmix, *, tm):
    m, d = x.shape
    rows = wmix.shape[1]
    assert tm % rows == 0
    sec = N_HEADS * HEAD_DIM
    row_blk = lambda i, j: (i, 0)
    const = lambda i, j: (0, 0)
    out_blk = pl.BlockSpec((tm, sec), row_blk)
    return pl.pallas_call(
        functools.partial(_even_in_kernel, rows=rows),
        grid=(m // tm, 5),
        in_specs=[
            pl.BlockSpec((tm, d), row_blk),
            pl.BlockSpec((1, d), const),
            pl.BlockSpec((d, sec), lambda i, j: (0, j)),
            pl.BlockSpec((1, HEAD_DIM), const),
            pl.BlockSpec((1, HEAD_DIM), const),
            pl.BlockSpec((1, sec), const),
            pl.BlockSpec(wmix.shape, lambda i, j: (0, 0, 0)),
            pl.BlockSpec((rows, sec), const),
        ],
        out_specs=[out_blk] * 5,
        out_shape=[
            jax.ShapeDtypeStruct((m, sec), BF16),
            jax.ShapeDtypeStruct((m, sec), F32),
            jax.ShapeDtypeStruct((m, sec), F32),
            jax.ShapeDtypeStruct((m, sec), BF16),
            jax.ShapeDtypeStruct((m, sec), F32),
        ],
        scratch_shapes=[pltpu.VMEM((tm, d), BF16), pltpu.VMEM((tm, sec), F32)],
        compiler_params=_params(("parallel", "arbitrary")),
        name="even_in",
    )(x, g.reshape(1, d), w, qg.reshape(1, HEAD_DIM), kg.reshape(1, HEAD_DIM),
      avg.reshape(1, sec), wmix, bmix)


def _log_keep(z):
    return -(jnp.maximum(z, 0.0) + jnp.log(1.0 + jnp.exp(-jnp.abs(z))))


def _suffix_sum_weights():
    row = jnp.arange(2 * PAGE)[:, None] % PAGE
    col = jnp.arange(2 * PAGE)[None, :]
    return jnp.where(col < PAGE, row > col, True).astype(BF16)


def _sb_tiles(qs, kbs, vbs, biases, cum_w, runs, mask):
    rows = qs[0].shape[0]
    zs = [lax.dot_general(q, kb, (((1,), (1,)), ((), ())), preferred_element_type=F32)
          * SB_SCALE + bias for q, kb, bias in zip(qs, kbs, biases)]
    keeps = [_log_keep(z) for z in zs]
    parts = []
    for keep in keeps:
        lk = keep if mask is None else jnp.where(mask, keep, 0.0)
        hi = lk.astype(BF16)
        lo = (lk - hi.astype(F32)).astype(BF16)
        parts.append(jnp.concatenate([hi, lo], axis=1))
    cum = _dot(jnp.concatenate(parts, axis=0), cum_w)
    outs, new_runs, run = [], [], None
    for c, (z, keep, vb) in enumerate(zip(zs, keeps, vbs)):
        run = run if runs[c] is None else runs[c]
        cum_c = cum[c * rows:(c + 1) * rows]
        a = jnp.exp(z + keep + cum_c[:, :PAGE] + run)
        if mask is not None:
            a = jnp.where(mask, a, 0.0)
        outs.append(_dot(a.astype(BF16), vb))
        run = run + cum_c[:, PAGE:]
        new_runs.append(run)
    return outs, new_runs


def _sb_prompt_kernel(bias_ref, q_ref, k_ref, v_ref, cw_ref, o_ref, kb_ref, vb_ref, *, heads):
    hb = pl.program_id(1)
    i = pl.program_id(2)

    @pl.when(i == 0)
    def _():
        kb_ref[...] = k_ref[...].astype(BF16)
        vb_ref[...] = v_ref[...].astype(BF16)

    cum_w = cw_ref[...]
    cols = [slice(hh * HEAD_DIM, (hh + 1) * HEAD_DIM) for hh in range(heads)]
    qs = [q_ref[:, cs] for cs in cols]
    biases = [bias_ref[hb * heads + hh] for hh in range(heads)]

    def blocks(start, carry, mask):
        accs, runs = carry
        ds, runs = _sb_tiles(qs, [kb_ref[pl.ds(start, PAGE), cs] for cs in cols],
                             [vb_ref[pl.ds(start, PAGE), cs] for cs in cols],
                             biases, cum_w, runs, mask)
        return tuple(acc + d for acc, d in zip(accs, ds)), tuple(runs)

    zero = (jnp.zeros((PAGE, HEAD_DIM), F32),) * heads
    below_diag = (lax.broadcasted_iota(jnp.int32, (PAGE, PAGE), 1)
                  < lax.broadcasted_iota(jnp.int32, (PAGE, PAGE), 0))
    carry = blocks(pl.multiple_of(i * PAGE, PAGE), (zero, zero), below_diag)

    def body(it, carry):
        return blocks(pl.multiple_of((i - 1 - it) * PAGE, PAGE), carry, None)

    accs, _ = lax.fori_loop(0, i, body, carry)
    for acc, cs in zip(accs, cols):
        o_ref[:, cs] = acc.astype(BF16)


def _sb_prompt(q, k, v, bias, *, batch, heads):
    m, width = q.shape
    t = m // batch
    nq = t // PAGE
    hw = heads * HEAD_DIM
    qo_spec = pl.BlockSpec((PAGE, hw), lambda b, h, i: (b * nq + i, h))
    kv_spec = pl.BlockSpec((t, hw), lambda b, h, i: (b, h))
    return pl.pallas_call(
        functools.partial(_sb_prompt_kernel, heads=heads),
        grid=(batch, width // hw, nq),
        in_specs=[pl.BlockSpec(memory_space=pltpu.SMEM), qo_spec, kv_spec, kv_spec,
                  pl.BlockSpec((2 * PAGE, 2 * PAGE), lambda b, h, i: (0, 0))],
        out_specs=qo_spec,
        out_shape=jax.ShapeDtypeStruct((m, width), BF16),
        scratch_shapes=[pltpu.VMEM((t, hw), BF16), pltpu.VMEM((t, hw), BF16)],
        compiler_params=_params(("parallel", "parallel", "arbitrary")),
        name="sb_prompt",
    )(bias, q, k, v, _suffix_sum_weights())


def _sb_sample_kernel(pt_ref, q_ref, bias_ref, cw_ref, knew_ref, vnew_ref, *rest, tq, pages):
    kpage_refs, vpage_refs = rest[:pages], rest[pages:2 * pages]
    o_ref, acc_ref, run_ref = rest[2 * pages:]
    j = pl.program_id(1)
    q = q_ref[0]
    bias = bias_ref[...]
    cum_w = cw_ref[...]

    @pl.when(j == 0)
    def _():
        t_pos = lax.rem(lax.broadcasted_iota(jnp.int32, (q.shape[0], PAGE), 0), tq)
        s_pos = lax.broadcasted_iota(jnp.int32, (q.shape[0], PAGE), 1)
        ds, runs = _sb_tiles([q], [knew_ref[0].astype(BF16)], [vnew_ref[0].astype(BF16)], [bias],
                             cum_w, [jnp.zeros((q.shape[0], PAGE), F32)], s_pos < t_pos)
        acc_ref[...] = ds[0]
        run_ref[...] = runs[0]

    @pl.when(j > 0)
    def _():
        ds, runs = _sb_tiles([q] * pages, [r[0].astype(BF16) for r in kpage_refs],
                             [r[0].astype(BF16) for r in vpage_refs], [bias] * pages, cum_w,
                             [run_ref[...]] + [None] * (pages - 1), None)
        acc_ref[...] += functools.reduce(lambda x, y: x + y, ds)
        run_ref[...] = runs[-1]

    @pl.when(j == pl.num_programs(1) - 1)
    def _():
        for h in range(N_HEADS):
            cs = slice(h * HEAD_DIM, (h + 1) * HEAD_DIM)
            o_ref[0, :, cs] = acc_ref[h * tq:(h + 1) * tq, cs].astype(BF16)


def _sb_sample(q, k_new, v_new, cache_k, cache_v, page_table, bias, *, batch, page_base, pages):
    m, width = q.shape
    tq = m // batch
    n_pages = page_table.shape[1]
    assert n_pages % pages == 0
    rows = N_HEADS * tq
    q4 = q.reshape(batch, tq, N_HEADS, HEAD_DIM).transpose(0, 2, 1, 3)
    eye = jnp.eye(N_HEADS, dtype=BF16)
    q_rows = (q4[:, :, :, None, :] * eye[None, :, None, :, None]).reshape(batch, rows, width)
    bias_rows = jnp.broadcast_to(jnp.repeat(bias, tq)[:, None], (rows, PAGE))
    pad = ((0, 0), (0, PAGE - tq), (0, 0))
    k_pad = jnp.pad(k_new.reshape(batch, tq, width), pad)
    v_pad = jnp.pad(v_new.reshape(batch, tq, width), pad)

    def page_spec(p):
        def index(b, j, pt):
            return (page_base + pt[b, n_pages - 1 - (jnp.maximum(j, 1) - 1) * pages - p], 0, 0)
        return pl.BlockSpec((1, PAGE, width), index)

    per_b = lambda b, j, pt: (b, 0, 0)
    const = lambda b, j, pt: (0, 0)
    new_spec = pl.BlockSpec((1, PAGE, width), per_b)
    page_specs = [page_spec(p) for p in range(pages)]
    out = pl.pallas_call(
        functools.partial(_sb_sample_kernel, tq=tq, pages=pages),
        grid_spec=pltpu.PrefetchScalarGridSpec(
            num_scalar_prefetch=1,
            grid=(batch, n_pages // pages + 1),
            in_specs=[
                pl.BlockSpec((1, rows, width), per_b),
                pl.BlockSpec((rows, PAGE), const),
                pl.BlockSpec((2 * PAGE, 2 * PAGE), const),
                new_spec, new_spec, *page_specs, *page_specs,
            ],
            out_specs=pl.BlockSpec((1, tq, width), per_b),
            scratch_shapes=[pltpu.VMEM((rows, width), F32), pltpu.VMEM((rows, PAGE), F32)],
        ),
        out_shape=jax.ShapeDtypeStruct((batch, tq, width), BF16),
        compiler_params=_params(("parallel", "arbitrary")),
        name="sb_sample",
    )(page_table, q_rows, bias_rows, _suffix_sum_weights(), k_pad, v_pad,
      *([cache_k] * pages), *([cache_v] * pages))
    return out.reshape(m, width)


def _out_proj_kernel(a_ref, b_ref, wa_ref, wb_ref, r_ref, o_ref):
    o_ref[...] = r_ref[...] + _dot(a_ref[...], wa_ref[...]) + _dot(b_ref[...], wb_ref[...])


def _out_proj(a, b, w, resid, *, tm, tn):
    m, ka = a.shape
    n = w.shape[1]
    assert b.shape[1] == ka and w.shape[0] == 2 * ka
    return pl.pallas_call(
        _out_proj_kernel,
        grid=(m // tm, n // tn),
        in_specs=[
            pl.BlockSpec((tm, ka), lambda i, j: (i, 0)),
            pl.BlockSpec((tm, ka), lambda i, j: (i, 0)),
            pl.BlockSpec((ka, tn), lambda i, j: (0, j)),
            pl.BlockSpec((ka, tn), lambda i, j: (1, j)),
            pl.BlockSpec((tm, tn), lambda i, j: (i, j)),
        ],
        out_specs=pl.BlockSpec((tm, tn), lambda i, j: (i, j)),
        out_shape=jax.ShapeDtypeStruct((m, n), F32),
        compiler_params=_params(("parallel", "parallel")),
        name="out_proj",
    )(a, b, w, w, resid)


def _odd_mix_kernel(p_ref, pool_prev_ref, conv_prev_ref, cwg_ref, cscale_ref, convw_ref,
                    c_ref, d_ref, pool_out_ref, conv_out_ref, xbuf_ref, ubuf_ref, *, tt, pos0):
    t_idx = pl.program_id(1)
    width = c_ref.shape[-1]

    @pl.when(t_idx == 0)
    def _():
        xbuf_ref[HIST_PAD - POOL_HIST:HIST_PAD, :] = pool_prev_ref[0]
        ubuf_ref[HIST_PAD - CONV_HIST:HIST_PAD, :] = conv_prev_ref[0]

    xc = p_ref[0, :, 0:width]
    gb = p_ref[0, :, width:2 * width]
    u = p_ref[0, :, 2 * width:3 * width] * p_ref[0, :, 3 * width:4 * width]
    xbuf_ref[HIST_PAD:HIST_PAD + tt, :] = xc
    ubuf_ref[HIST_PAD:HIST_PAD + tt, :] = u

    seen = pos0 + t_idx * tt + lax.broadcasted_iota(jnp.int32, (tt, 1), 0) + 1
    for grp, win in enumerate(POOL_WINDOWS):
        cs = slice(grp * POOL_GROUP, (grp + 1) * POOL_GROUP)
        s = xc[:, cs]
        for back in range(1, win):
            s = s + xbuf_ref[HIST_PAD - back:HIST_PAD - back + tt, cs]
        count = jnp.minimum(seen, win).astype(F32)
        pooled = s / count - xc[:, cs]
        mixed = _dot(pooled.astype(BF16), cwg_ref[grp]) * cscale_ref[:, cs]
        c_ref[0, :, cs] = mixed.astype(BF16)

    conv = (ubuf_ref[HIST_PAD - 2:HIST_PAD - 2 + tt, :] * convw_ref[0:1, :]
            + ubuf_ref[HIST_PAD - 1:HIST_PAD - 1 + tt, :] * convw_ref[1:2, :]
            + u * convw_ref[2:3, :])
    d_ref[0] = (gb * conv).astype(BF16)

    pool_tail = xbuf_ref[HIST_PAD + tt - POOL_HIST:HIST_PAD + tt, :]
    conv_tail = ubuf_ref[HIST_PAD + tt - CONV_HIST:HIST_PAD + tt, :]
    pool_out_ref[0] = pool_tail
    conv_out_ref[0] = conv_tail
    xbuf_ref[HIST_PAD - POOL_HIST:HIST_PAD, :] = pool_tail
    ubuf_ref[HIST_PAD - CONV_HIST:HIST_PAD, :] = conv_tail


def _odd_mix(p, pool_prev, conv_prev, cwg, cscale, convw, *, tt, pos0):
    batch, t, _ = p.shape
    width = pool_prev.shape[-1]
    seq_blk = lambda b, i: (b, i, 0)
    per_b = lambda b, i: (b, 0, 0)
    return pl.pallas_call(
        functools.partial(_odd_mix_kernel, tt=tt, pos0=pos0),
        grid=(batch, t // tt),
        in_specs=[
            pl.BlockSpec((1, tt, 4 * width), seq_blk),
            pl.BlockSpec((1, POOL_HIST, width), per_b),
            pl.BlockSpec((1, CONV_HIST, width), per_b),
            pl.BlockSpec(cwg.shape, lambda b, i: (0, 0, 0)),
            pl.BlockSpec((1, width), lambda b, i: (0, 0)),
            pl.BlockSpec(convw.shape, lambda b, i: (0, 0)),
        ],
        out_specs=[
            pl.BlockSpec((1, tt, width), seq_blk),
            pl.BlockSpec((1, tt, width), seq_blk),
            pl.BlockSpec((1, POOL_HIST, width), per_b),
            pl.BlockSpec((1, CONV_HIST, width), per_b),
        ],
        out_shape=[
            jax.ShapeDtypeStruct((batch, t, width), BF16),
            jax.ShapeDtypeStruct((batch, t, width), BF16),
            jax.ShapeDtypeStruct((batch, POOL_HIST, width), F32),
            jax.ShapeDtypeStruct((batch, CONV_HIST, width), F32),
        ],
        scratch_shapes=[pltpu.VMEM((HIST_PAD + tt, width), F32),
                        pltpu.VMEM((HIST_PAD + tt, width), F32)],
        compiler_params=_params(("parallel", "arbitrary")),
        name="odd_mix",
    )(p, pool_prev, conv_prev, cwg, cscale.reshape(1, width), convw)


def _pick_tile(m, target):
    return target if m % target == 0 else m


def kernel(x_prompt, x_sample, cache_k, cache_v, state_pool, state_conv, page_table, norm_g, ffn_w_gate, ffn_w_up, ffn_w_down, w_in_even, w_out_even, q_norm_g, k_norm_g, sb_bias, a_v_norm_g, a_ws, a_bs, w_in_odd, w_out_odd, c_wg, c_scale, d_conv_w):
    bp, tp, d = x_prompt.shape
    bs, ts, _ = x_sample.shape
    depth = norm_g.shape[0]
    mp, ms = bp * tp, bs * ts
    sec = N_HEADS * HEAD_DIM

    tm_p = _pick_tile(mp, 512)
    tf = _pick_tile(ffn_w_gate.shape[-1], 512)
    n_out = w_out_even.shape[-1]

    def ffn_pair(yp, ys, layer, half):
        g = norm_g[layer, 2 * half]
        wg, wu, wd = (w[layer, half].astype(BF16) for w in (ffn_w_gate, ffn_w_up, ffn_w_down))
        return (_ffn(yp, g, wg, wu, wd, tm=tm_p, tf=tf), _ffn(ys, g, wg, wu, wd, tm=ms, tf=tf))

    yp = x_prompt.reshape(mp, d)
    ys = x_sample.reshape(ms, d)
    outs = {name: [] for name in ("kp", "vp", "ks", "vs", "av", "pp", "ps", "cp", "cs")}

    for layer in range(depth):
        i = layer // 2
        yp, ys = ffn_pair(yp, ys, layer, 0)
        if layer % 2 == 0:
            w_in_b, w_out_b = w_in_even[i].astype(BF16), w_out_even[i].astype(BF16)
            tril_p = jnp.tril(a_ws[i]).astype(BF16)
            bias_p = jnp.repeat(a_bs[i].T, HEAD_DIM, axis=1)
            tril_s = jnp.tril(a_ws[i][:, :ts, :ts])
            eye = jnp.eye(bs, dtype=F32)
            wmix_s = (eye[None, :, None, :, None] * tril_s[:, None, :, None, :]).reshape(
                N_HEADS, ms, ms).astype(BF16)
            wmix_s = jnp.pad(wmix_s, ((0, 0), (0, 0), (0, max(0, HEAD_DIM - ms))))
            bias_s = jnp.tile(jnp.repeat(a_bs[i][:, :ts].T, HEAD_DIM, axis=1), (bs, 1))

            q, k, v, gated, _ = _even_in(yp, norm_g[layer, 1], w_in_b, q_norm_g[i],
                                         k_norm_g[i], a_v_norm_g[i], tril_p, bias_p, tm=tm_p)
            sb = _sb_prompt(q, k, v, sb_bias[i], batch=bp, heads=8)
            yp = _out_proj(gated, sb, w_out_b, yp, tm=tm_p, tn=n_out)
            outs["kp"].append(k.reshape(bp, tp, N_HEADS, HEAD_DIM))
            outs["vp"].append(v.reshape(bp, tp, N_HEADS, HEAD_DIM))

            q, k, v, gated, avn = _even_in(ys, norm_g[layer, 1], w_in_b, q_norm_g[i],
                                           k_norm_g[i], a_v_norm_g[i], wmix_s, bias_s, tm=ms)
            n_phys = cache_k.shape[1]
            pool_shape = (cache_k.shape[0] * n_phys, PAGE, sec)
            sb = _sb_sample(q, k, v, cache_k.reshape(pool_shape), cache_v.reshape(pool_shape),
                            page_table, sb_bias[i], batch=bs, page_base=i * n_phys, pages=8)
            ys = _out_proj(gated, sb, w_out_b, ys, tm=ms, tn=n_out)
            outs["ks"].append(k.reshape(bs, ts, N_HEADS, HEAD_DIM))
            outs["vs"].append(v.reshape(bs, ts, N_HEADS, HEAD_DIM))
            outs["av"].append(avn.reshape(bs, ts, sec))
        else:
            width = c_scale.shape[-1]
            w_in_b, w_out_b = w_in_odd[i].astype(BF16), w_out_odd[i].astype(BF16)
            cwg_b = c_wg[i].astype(BF16)
            p = _norm_matmul(yp, norm_g[layer, 1], w_in_b, tm=tm_p, tn=1024)
            c_out, d_out, pool_st, conv_st = _odd_mix(
                p.reshape(bp, tp, 4 * width),
                jnp.zeros((bp, POOL_HIST, width), F32), jnp.zeros((bp, CONV_HIST, width), F32),
                cwg_b, c_scale[i], d_conv_w[i], tt=_pick_tile(tp, 512), pos0=0)
            yp = _out_proj(c_out.reshape(mp, width), d_out.reshape(mp, width), w_out_b,
                           yp, tm=tm_p, tn=n_out)
            outs["pp"].append(pool_st)
            outs["cp"].append(conv_st)

            p = _norm_matmul(ys, norm_g[layer, 1], w_in_b, tm=ms, tn=1024)
            past = page_table.shape[1] * PAGE
            c_out, d_out, pool_st, conv_st = _odd_mix(
                p.reshape(bs, ts, 4 * width), state_pool[i], state_conv[i],
                cwg_b, c_scale[i], d_conv_w[i], tt=ts, pos0=past)
            ys = _out_proj(c_out.reshape(ms, width), d_out.reshape(ms, width), w_out_b,
                           ys, tm=ms, tn=n_out)
            outs["ps"].append(pool_st)
            outs["cs"].append(conv_st)
        yp, ys = ffn_pair(yp, ys, layer, 1)

    return (yp.reshape(bp, tp, d), ys.reshape(bs, ts, d),
            jnp.stack(outs["kp"]), jnp.stack(outs["vp"]),
            jnp.stack(outs["ks"]), jnp.stack(outs["vs"]), jnp.stack(outs["av"]),
            jnp.stack(outs["pp"]), jnp.stack(outs["ps"]),
            jnp.stack(outs["cp"]), jnp.stack(outs["cs"]))
```

```python
import functools

import jax
import jax.numpy as jnp
from jax import lax
from jax.experimental import pallas as pl
from jax.experimental.pallas import tpu as pltpu

F32 = jnp.float32
BF16 = jnp.bfloat16

EPS = 1e-6
HEAD_DIM = 128
N_HEADS = 8
SB_SCALE = HEAD_DIM ** -0.5
PAGE = 128
POOL_WINDOWS = (2, 4, 8, 16)
POOL_GROUP = 256
POOL_HIST = 15
CONV_HIST = 2
HIST_PAD = 16
VMEM_LIMIT = 56 * 1024 * 1024


def _params(sem):
    return pltpu.CompilerParams(dimension_semantics=sem, vmem_limit_bytes=VMEM_LIMIT)


def _rms(x, g):
    return x * lax.rsqrt(jnp.mean(x * x, axis=-1, keepdims=True) + EPS) * g


def _dot(a, b):
    return jnp.dot(a, b, preferred_element_type=F32)


def _ffn_kernel(x_ref, xs_ref, g_ref, wg_ref, wu_ref, wd_ref, o_ref, os_ref, h_ref, hs_ref):
    i = pl.program_id(0)
    j = pl.program_id(1)

    @pl.when(j == 0)
    def _():
        x = x_ref[...]
        h_ref[...] = _rms(x, g_ref[...]).astype(BF16)
        o_ref[...] = x

    wg = wg_ref[...].astype(BF16)
    wu = wu_ref[...].astype(BF16)
    wd = wd_ref[...].astype(BF16)

    def half_swiglu(h):
        gate = _dot(h, wg)
        up = _dot(h, wu)
        return 0.5 * _dot((gate * jax.nn.sigmoid(gate) * up).astype(BF16), wd)

    o_ref[...] += half_swiglu(h_ref[...])

    @pl.when(i == 0)
    def _():
        @pl.when(j == 0)
        def _():
            hs_ref[...] = _rms(xs_ref[...], g_ref[...]).astype(BF16)
            os_ref[...] = xs_ref[...]

        os_ref[...] += half_swiglu(hs_ref[...])


def _ffn(x, xs, g, wg, wu, wd, *, layer, half, tm, tf):
    m, d = x.shape
    n_small = xs.shape[0]
    f = wg.shape[-1]
    rows = lambda i, j: (i, 0)
    const = lambda i, j: (0, 0)
    return pl.pallas_call(
        _ffn_kernel,
        grid=(m // tm, f // tf),
        in_specs=[
            pl.BlockSpec((tm, d), rows, pipeline_mode=pl.Buffered(1)),
            pl.BlockSpec((n_small, d), const),
            pl.BlockSpec((1, d), const),
            pl.BlockSpec((None, None, d, tf), lambda i, j: (layer, half, 0, j)),
            pl.BlockSpec((None, None, d, tf), lambda i, j: (layer, half, 0, j)),
            pl.BlockSpec((None, None, tf, d), lambda i, j: (layer, half, j, 0)),
        ],
        out_specs=[pl.BlockSpec((tm, d), rows), pl.BlockSpec((n_small, d), const)],
        out_shape=[jax.ShapeDtypeStruct((m, d), F32), jax.ShapeDtypeStruct((n_small, d), F32)],
        scratch_shapes=[pltpu.VMEM((tm, d), BF16), pltpu.VMEM((n_small, d), BF16)],
        compiler_params=_params(("arbitrary", "arbitrary")),
        name="ffn",
    )(x, xs, g.reshape(1, d), wg, wu, wd)


def _norm_matmul_kernel(x_ref, g_ref, w_ref, o_ref, h_ref):
    @pl.when(pl.program_id(1) == 0)
    def _():
        h_ref[...] = _rms(x_ref[...], g_ref[...]).astype(BF16)

    o_ref[...] = _dot(h_ref[...], w_ref[...])


def _norm_matmul(x, g, w, *, tm, tn):
    m, d = x.shape
    n = w.shape[1]
    return pl.pallas_call(
        _norm_matmul_kernel,
        grid=(m // tm, n // tn),
        in_specs=[
            pl.BlockSpec((tm, d), lambda i, j: (i, 0)),
            pl.BlockSpec((1, d), lambda i, j: (0, 0)),
            pl.BlockSpec((d, tn), lambda i, j: (0, j)),
        ],
        out_specs=pl.BlockSpec((tm, tn), lambda i, j: (i, j)),
        out_shape=jax.ShapeDtypeStruct((m, n), F32),
        scratch_shapes=[pltpu.VMEM((tm, d), BF16)],
        compiler_params=_params(("parallel", "arbitrary")),
        name="norm_matmul",
    )(x, g.reshape(1, d), w)


def _head_norm(p, g):
    cols = []
    for h in range(N_HEADS):
        cols.append(_rms(p[:, h * HEAD_DIM:(h + 1) * HEAD_DIM], g))
    return jnp.concatenate(cols, axis=-1)


def _even_in_kernel(x_ref, g_ref, w_ref, qg_ref, kg_ref, avg_ref, wmix_ref, bmix_ref,
                    q_ref, k_ref, v_ref, gated_ref, avn_ref, h_ref, au_ref, *, rows):
    j = pl.program_id(1)

    @pl.when(j == 0)
    def _():
        h_ref[...] = _rms(x_ref[...], g_ref[...]).astype(BF16)

    p = _dot(h_ref[...], w_ref[...])

    @pl.when(j == 0)
    def _():
        q_ref[...] = _head_norm(p, qg_ref[...]).astype(BF16)

    @pl.when(j == 1)
    def _():
        k_ref[...] = _head_norm(p, kg_ref[...])

    @pl.when(j == 2)
    def _():
        v_ref[...] = p

    @pl.when(j == 3)
    def _():
        au_ref[...] = jax.nn.gelu(p)

    @pl.when(j == 4)
    def _():
        avn = _rms(jax.nn.gelu(p), avg_ref[...])
        avn_ref[...] = avn
        avb = avn.astype(BF16)
        k_pad = wmix_ref.shape[2] - rows
        for c in range(p.shape[0] // rows):
            rs = slice(c * rows, (c + 1) * rows)
            for grp in range(N_HEADS):
                cs = slice(grp * HEAD_DIM, (grp + 1) * HEAD_DIM)
                blk = avb[rs, cs]
                if k_pad:
                    blk = jnp.concatenate([blk, jnp.zeros((k_pad, HEAD_DIM), BF16)], axis=0)
                mix = _dot(wmix_ref[grp], blk) + bmix_ref[:, cs]
                gated_ref[rs, cs] = (au_ref[rs, cs] * mix).astype(BF16)


def _even_in(x, g, w, qg, kg, avg, wmix, bmix, *, tm):
    m, d = x.shape
    rows = wmix.shape[1]
    assert tm % rows == 0
    sec = N_HEADS * HEAD_DIM
    row_blk = lambda i, j: (i, 0)
    const = lambda i, j: (0, 0)
    out_blk = pl.BlockSpec((tm, sec), row_blk)
    return pl.pallas_call(
        functools.partial(_even_in_kernel, rows=rows),
        grid=(m // tm, 5),
        in_specs=[
            pl.BlockSpec((tm, d), row_blk),
            pl.BlockSpec((1, d), const),
            pl.BlockSpec((d, sec), lambda i, j: (0, j)),
            pl.BlockSpec((1, HEAD_DIM), const),
            pl.BlockSpec((1, HEAD_DIM), const),
            pl.BlockSpec((1, sec), const),
            pl.BlockSpec(wmix.shape, lambda i, j: (0, 0, 0)),
            pl.BlockSpec((rows, sec), const),
        ],
        out_specs=[out_blk] * 5,
        out_shape=[
            jax.ShapeDtypeStruct((m, sec), BF16),
            jax.ShapeDtypeStruct((m, sec), F32),
            jax.ShapeDtypeStruct((m, sec), F32),
            jax.ShapeDtypeStruct((m, sec), BF16),
            jax.ShapeDtypeStruct((m, sec), F32),
        ],
        scratch_shapes=[pltpu.VMEM((tm, d), BF16), pltpu.VMEM((tm, sec), F32)],
        compiler_params=_params(("parallel", "arbitrary")),
        name="even_in",
    )(x, g.reshape(1, d), w, qg.reshape(1, HEAD_DIM), kg.reshape(1, HEAD_DIM),
      avg.reshape(1, sec), wmix, bmix)


def _log_keep(z):
    return -(jnp.maximum(z, 0.0) + jnp.log(1.0 + jnp.exp(-jnp.abs(z))))


def _suffix_sum_weights():
    row = jnp.arange(2 * PAGE)[:, None] % PAGE
    col = jnp.arange(2 * PAGE)[None, :]
    return jnp.where(col < PAGE, row > col, True).astype(BF16)


def _sb_tiles(qs, kbs, vbs, biases, cum_w, runs, mask):
    rows = qs[0].shape[0]
    zs = [lax.dot_general(q, kb, (((1,), (1,)), ((), ())), preferred_element_type=F32)
          * SB_SCALE + bias for q, kb, bias in zip(qs, kbs, biases)]
    keeps = [_log_keep(z) for z in zs]
    parts = []
    for keep in keeps:
        lk = keep if mask is None else jnp.where(mask, keep, 0.0)
        hi = lk.astype(BF16)
        lo = (lk - hi.astype(F32)).astype(BF16)
        parts.append(jnp.concatenate([hi, lo], axis=1))
    cum = _dot(jnp.concatenate(parts, axis=0), cum_w)
    outs, new_runs, run = [], [], None
    for c, (z, keep, vb) in enumerate(zip(zs, keeps, vbs)):
        run = run if runs[c] is None else runs[c]
        cum_c = cum[c * rows:(c + 1) * rows]
        a = jnp.exp(z + keep + cum_c[:, :PAGE] + run)
        if mask is not None:
            a = jnp.where(mask, a, 0.0)
        outs.append(_dot(a.astype(BF16), vb))
        run = run + cum_c[:, PAGE:]
        new_runs.append(run)
    return outs, new_runs


def _sb_prompt_kernel(bias_ref, q_ref, k_ref, v_ref, cw_ref, o_ref, kb_ref, vb_ref, *, heads):
    hb = pl.program_id(1)
    i = pl.program_id(2)

    @pl.when(i == 0)
    def _():
        kb_ref[...] = k_ref[...].astype(BF16)
        vb_ref[...] = v_ref[...].astype(BF16)

    cum_w = cw_ref[...]
    cols = [slice(hh * HEAD_DIM, (hh + 1) * HEAD_DIM) for hh in range(heads)]
    qs = [q_ref[:, cs] for cs in cols]
    biases = [bias_ref[hb * heads + hh] for hh in range(heads)]

    def blocks(start, carry, mask):
        accs, runs = carry
        ds, runs = _sb_tiles(qs, [kb_ref[pl.ds(start, PAGE), cs] for cs in cols],
                             [vb_ref[pl.ds(start, PAGE), cs] for cs in cols],
                             biases, cum_w, runs, mask)
        return tuple(acc + d for acc, d in zip(accs, ds)), tuple(runs)

    zero = (jnp.zeros((PAGE, HEAD_DIM), F32),) * heads
    below_diag = (lax.broadcasted_iota(jnp.int32, (PAGE, PAGE), 1)
                  < lax.broadcasted_iota(jnp.int32, (PAGE, PAGE), 0))
    carry = blocks(pl.multiple_of(i * PAGE, PAGE), (zero, zero), below_diag)

    def body(it, carry):
        return blocks(pl.multiple_of((i - 1 - it) * PAGE, PAGE), carry, None)

    accs, _ = lax.fori_loop(0, i, body, carry)
    for acc, cs in zip(accs, cols):
        o_ref[:, cs] = acc.astype(BF16)


def _sb_prompt(q, k, v, bias, *, batch, heads):
    m, width = q.shape
    t = m // batch
    nq = t // PAGE
    hw = heads * HEAD_DIM
    qo_spec = pl.BlockSpec((PAGE, hw), lambda b, h, i: (b * nq + i, h))
    kv_spec = pl.BlockSpec((t, hw), lambda b, h, i: (b, h))
    return pl.pallas_call(
        functools.partial(_sb_prompt_kernel, heads=heads),
        grid=(batch, width // hw, nq),
        in_specs=[pl.BlockSpec(memory_space=pltpu.SMEM), qo_spec, kv_spec, kv_spec,
                  pl.BlockSpec((2 * PAGE, 2 * PAGE), lambda b, h, i: (0, 0))],
        out_specs=qo_spec,
        out_shape=jax.ShapeDtypeStruct((m, width), BF16),
        scratch_shapes=[pltpu.VMEM((t, hw), BF16), pltpu.VMEM((t, hw), BF16)],
        compiler_params=_params(("parallel", "parallel", "arbitrary")),
        name="sb_prompt",
    )(bias, q, k, v, _suffix_sum_weights())


def _sb_sample_kernel(pt_ref, q_ref, bias_ref, cw_ref, knew_ref, vnew_ref, *rest, tq, pages):
    kpage_refs, vpage_refs = rest[:pages], rest[pages:2 * pages]
    o_ref, acc_ref, run_ref = rest[2 * pages:]
    j = pl.program_id(1)
    q = q_ref[0]
    bias = bias_ref[...]
    cum_w = cw_ref[...]

    @pl.when(j == 0)
    def _():
        t_pos = lax.rem(lax.broadcasted_iota(jnp.int32, (q.shape[0], PAGE), 0), tq)
        s_pos = lax.broadcasted_iota(jnp.int32, (q.shape[0], PAGE), 1)
        ds, runs = _sb_tiles([q], [knew_ref[0].astype(BF16)], [vnew_ref[0].astype(BF16)], [bias],
                             cum_w, [jnp.zeros((q.shape[0], PAGE), F32)], s_pos < t_pos)
        acc_ref[...] = ds[0]
        run_ref[...] = runs[0]

    @pl.when(j > 0)
    def _():
        def page(ref):
            heads = [ref[0, pl.ds(h, PAGE, stride=N_HEADS), :] for h in range(N_HEADS)]
            return jnp.concatenate(heads, axis=1).astype(BF16)

        ds, runs = _sb_tiles([q] * pages, [page(r) for r in kpage_refs],
                             [page(r) for r in vpage_refs], [bias] * pages, cum_w,
                             [run_ref[...]] + [None] * (pages - 1), None)
        acc_ref[...] += functools.reduce(lambda x, y: x + y, ds)
        run_ref[...] = runs[-1]

    @pl.when(j == pl.num_programs(1) - 1)
    def _():
        for h in range(N_HEADS):
            cs = slice(h * HEAD_DIM, (h + 1) * HEAD_DIM)
            o_ref[0, :, cs] = acc_ref[h * tq:(h + 1) * tq, cs].astype(BF16)


def _sb_sample(q, k_new, v_new, cache_k, cache_v, page_table, bias, *, batch, page_base, pages):
    m, width = q.shape
    tq = m // batch
    n_pages = page_table.shape[1]
    assert n_pages % pages == 0
    rows = N_HEADS * tq
    q4 = q.reshape(batch, tq, N_HEADS, HEAD_DIM).transpose(0, 2, 1, 3)
    eye = jnp.eye(N_HEADS, dtype=BF16)
    q_rows = (q4[:, :, :, None, :] * eye[None, :, None, :, None]).reshape(batch, rows, width)
    bias_rows = jnp.broadcast_to(jnp.repeat(bias, tq)[:, None], (rows, PAGE))
    pad = ((0, 0), (0, PAGE - tq), (0, 0))
    k_pad = jnp.pad(k_new.reshape(batch, tq, width), pad)
    v_pad = jnp.pad(v_new.reshape(batch, tq, width), pad)

    def page_spec(p):
        def index(b, j, pt):
            return (page_base + pt[b, n_pages - 1 - (jnp.maximum(j, 1) - 1) * pages - p], 0, 0)
        return pl.BlockSpec((1, PAGE * N_HEADS, HEAD_DIM), index)

    per_b = lambda b, j, pt: (b, 0, 0)
    const = lambda b, j, pt: (0, 0)
    new_spec = pl.BlockSpec((1, PAGE, width), per_b)
    page_specs = [page_spec(p) for p in range(pages)]
    out = pl.pallas_call(
        functools.partial(_sb_sample_kernel, tq=tq, pages=pages),
        grid_spec=pltpu.PrefetchScalarGridSpec(
            num_scalar_prefetch=1,
            grid=(batch, n_pages // pages + 1),
            in_specs=[
                pl.BlockSpec((1, rows, width), per_b),
                pl.BlockSpec((rows, PAGE), const),
                pl.BlockSpec((2 * PAGE, 2 * PAGE), const),
                new_spec, new_spec, *page_specs, *page_specs,
            ],
            out_specs=pl.BlockSpec((1, tq, width), per_b),
            scratch_shapes=[pltpu.VMEM((rows, width), F32), pltpu.VMEM((rows, PAGE), F32)],
        ),
        out_shape=jax.ShapeDtypeStruct((batch, tq, width), BF16),
        compiler_params=_params(("parallel", "arbitrary")),
        name="sb_sample",
    )(page_table, q_rows, bias_rows, _suffix_sum_weights(), k_pad, v_pad,
      *([cache_k] * pages), *([cache_v] * pages))
    return out.reshape(m, width)


def _out_proj_kernel(a_ref, b_ref, wa_ref, wb_ref, r_ref, o_ref):
    o_ref[...] = r_ref[...] + _dot(a_ref[...], wa_ref[...]) + _dot(b_ref[...], wb_ref[...])


def _out_proj(a, b, w, resid, *, tm, tn):
    m, ka = a.shape
    n = w.shape[1]
    assert b.shape[1] == ka and w.shape[0] == 2 * ka
    return pl.pallas_call(
        _out_proj_kernel,
        grid=(m // tm, n // tn),
        in_specs=[
            pl.BlockSpec((tm, ka), lambda i, j: (i, 0)),
            pl.BlockSpec((tm, ka), lambda i, j: (i, 0)),
            pl.BlockSpec((ka, tn), lambda i, j: (0, j)),
            pl.BlockSpec((ka, tn), lambda i, j: (1, j)),
            pl.BlockSpec((tm, tn), lambda i, j: (i, j)),
        ],
        out_specs=pl.BlockSpec((tm, tn), lambda i, j: (i, j)),
        out_shape=jax.ShapeDtypeStruct((m, n), F32),
        compiler_params=_params(("parallel", "parallel")),
        name="out_proj",
    )(a, b, w, w, resid)


def _odd_mix_kernel(p_ref, pool_prev_ref, conv_prev_ref, cwg_ref, cscale_ref, convw_ref,
                    c_ref, d_ref, pool_out_ref, conv_out_ref, xbuf_ref, ubuf_ref, *, tt, pos0):
    t_idx = pl.program_id(1)
    width = c_ref.shape[-1]

    @pl.when(t_idx == 0)
    def _():
        xbuf_ref[HIST_PAD - POOL_HIST:HIST_PAD, :] = pool_prev_ref[0]
        ubuf_ref[HIST_PAD - CONV_HIST:HIST_PAD, :] = conv_prev_ref[0]

    xc = p_ref[0, :, 0:width]
    gb = p_ref[0, :, width:2 * width]
    u = p_ref[0, :, 2 * width:3 * width] * p_ref[0, :, 3 * width:4 * width]
    xbuf_ref[HIST_PAD:HIST_PAD + tt, :] = xc
    ubuf_ref[HIST_PAD:HIST_PAD + tt, :] = u

    seen = pos0 + t_idx * tt + lax.broadcasted_iota(jnp.int32, (tt, 1), 0) + 1
    for grp, win in enumerate(POOL_WINDOWS):
        cs = slice(grp * POOL_GROUP, (grp + 1) * POOL_GROUP)
        s = xc[:, cs]
        for back in range(1, win):
            s = s + xbuf_ref[HIST_PAD - back:HIST_PAD - back + tt, cs]
        count = jnp.minimum(seen, win).astype(F32)
        pooled = s / count - xc[:, cs]
        mixed = _dot(pooled.astype(BF16), cwg_ref[grp]) * cscale_ref[:, cs]
        c_ref[0, :, cs] = mixed.astype(BF16)

    conv = (ubuf_ref[HIST_PAD - 2:HIST_PAD - 2 + tt, :] * convw_ref[0:1, :]
            + ubuf_ref[HIST_PAD - 1:HIST_PAD - 1 + tt, :] * convw_ref[1:2, :]
            + u * convw_ref[2:3, :])
    d_ref[0] = (gb * conv).astype(BF16)

    pool_tail = xbuf_ref[HIST_PAD + tt - POOL_HIST:HIST_PAD + tt, :]
    conv_tail = ubuf_ref[HIST_PAD + tt - CONV_HIST:HIST_PAD + tt, :]
    pool_out_ref[0] = pool_tail
    conv_out_ref[0] = conv_tail
    xbuf_ref[HIST_PAD - POOL_HIST:HIST_PAD, :] = pool_tail
    ubuf_ref[HIST_PAD - CONV_HIST:HIST_PAD, :] = conv_tail


def _odd_mix(p, pool_prev, conv_prev, cwg, cscale, convw, *, tt, pos0):
    batch, t, _ = p.shape
    width = pool_prev.shape[-1]
    seq_blk = lambda b, i: (b, i, 0)
    per_b = lambda b, i: (b, 0, 0)
    return pl.pallas_call(
        functools.partial(_odd_mix_kernel, tt=tt, pos0=pos0),
        grid=(batch, t // tt),
        in_specs=[
            pl.BlockSpec((1, tt, 4 * width), seq_blk),
            pl.BlockSpec((1, POOL_HIST, width), per_b),
            pl.BlockSpec((1, CONV_HIST, width), per_b),
            pl.BlockSpec(cwg.shape, lambda b, i: (0, 0, 0)),
            pl.BlockSpec((1, width), lambda b, i: (0, 0)),
            pl.BlockSpec(convw.shape, lambda b, i: (0, 0)),
        ],
        out_specs=[
            pl.BlockSpec((1, tt, width), seq_blk),
            pl.BlockSpec((1, tt, width), seq_blk),
            pl.BlockSpec((1, POOL_HIST, width), per_b),
            pl.BlockSpec((1, CONV_HIST, width), per_b),
        ],
        out_shape=[
            jax.ShapeDtypeStruct((batch, t, width), BF16),
            jax.ShapeDtypeStruct((batch, t, width), BF16),
            jax.ShapeDtypeStruct((batch, POOL_HIST, width), F32),
            jax.ShapeDtypeStruct((batch, CONV_HIST, width), F32),
        ],
        scratch_shapes=[pltpu.VMEM((HIST_PAD + tt, width), F32),
                        pltpu.VMEM((HIST_PAD + tt, width), F32)],
        compiler_params=_params(("parallel", "arbitrary")),
        name="odd_mix",
    )(p, pool_prev, conv_prev, cwg, cscale.reshape(1, width), convw)


def _pick_tile(m, target):
    return target if m % target == 0 else m


def kernel(x_prompt, x_sample, cache_k, cache_v, state_pool, state_conv, page_table, norm_g, ffn_w_gate, ffn_w_up, ffn_w_down, w_in_even, w_out_even, q_norm_g, k_norm_g, sb_bias, a_v_norm_g, a_ws, a_bs, w_in_odd, w_out_odd, c_wg, c_scale, d_conv_w):
    bp, tp, d = x_prompt.shape
    bs, ts, _ = x_sample.shape
    depth = norm_g.shape[0]
    mp, ms = bp * tp, bs * ts
    sec = N_HEADS * HEAD_DIM

    tm_p = _pick_tile(mp, 512)
    n_out = w_out_even.shape[-1]

    def ffn_pair(yp, ys, layer, half):
        return _ffn(yp, ys, norm_g[layer, 2 * half], ffn_w_gate, ffn_w_up, ffn_w_down,
                    layer=layer, half=half, tm=_pick_tile(mp, 1024),
                    tf=_pick_tile(ffn_w_gate.shape[-1], 256))

    yp = x_prompt.reshape(mp, d)
    ys = x_sample.reshape(ms, d)
    outs = {name: [] for name in ("kp", "vp", "ks", "vs", "av", "pp", "ps", "cp", "cs")}

    for layer in range(depth):
        i = layer // 2
        yp, ys = ffn_pair(yp, ys, layer, 0)
        if layer % 2 == 0:
            w_in_b, w_out_b = w_in_even[i].astype(BF16), w_out_even[i].astype(BF16)
            tril_p = jnp.tril(a_ws[i]).astype(BF16)
            bias_p = jnp.repeat(a_bs[i].T, HEAD_DIM, axis=1)
            tril_s = jnp.tril(a_ws[i][:, :ts, :ts])
            eye = jnp.eye(bs, dtype=F32)
            wmix_s = (eye[None, :, None, :, None] * tril_s[:, None, :, None, :]).reshape(
                N_HEADS, ms, ms).astype(BF16)
            wmix_s = jnp.pad(wmix_s, ((0, 0), (0, 0), (0, max(0, HEAD_DIM - ms))))
            bias_s = jnp.tile(jnp.repeat(a_bs[i][:, :ts].T, HEAD_DIM, axis=1), (bs, 1))

            q, k, v, gated, _ = _even_in(yp, norm_g[layer, 1], w_in_b, q_norm_g[i],
                                         k_norm_g[i], a_v_norm_g[i], tril_p, bias_p, tm=tm_p)
            sb = _sb_prompt(q, k, v, sb_bias[i], batch=bp, heads=8)
            yp = _out_proj(gated, sb, w_out_b, yp, tm=tm_p, tn=n_out)
            outs["kp"].append(k.reshape(bp, tp, N_HEADS, HEAD_DIM))
            outs["vp"].append(v.reshape(bp, tp, N_HEADS, HEAD_DIM))

            q, k, v, gated, avn = _even_in(ys, norm_g[layer, 1], w_in_b, q_norm_g[i],
                                           k_norm_g[i], a_v_norm_g[i], wmix_s, bias_s, tm=ms)
            n_phys = cache_k.shape[1]
            pool_shape = (cache_k.shape[0] * n_phys, PAGE * N_HEADS, HEAD_DIM)
            sb = _sb_sample(q, k, v, cache_k.reshape(pool_shape), cache_v.reshape(pool_shape),
                            page_table, sb_bias[i], batch=bs, page_base=i * n_phys, pages=8)
            ys = _out_proj(gated, sb, w_out_b, ys, tm=ms, tn=n_out)
            outs["ks"].append(k.reshape(bs, ts, N_HEADS, HEAD_DIM))
            outs["vs"].append(v.reshape(bs, ts, N_HEADS, HEAD_DIM))
            outs["av"].append(avn.reshape(bs, ts, sec))
        else:
            width = c_scale.shape[-1]
            w_in_b, w_out_b = w_in_odd[i].astype(BF16), w_out_odd[i].astype(BF16)
            cwg_b = c_wg[i].astype(BF16)
            p = _norm_matmul(yp, norm_g[layer, 1], w_in_b, tm=tm_p, tn=1024)
            c_out, d_out, pool_st, conv_st = _odd_mix(
                p.reshape(bp, tp, 4 * width),
                jnp.zeros((bp, POOL_HIST, width), F32), jnp.zeros((bp, CONV_HIST, width), F32),
                cwg_b, c_scale[i], d_conv_w[i], tt=_pick_tile(tp, 512), pos0=0)
            yp = _out_proj(c_out.reshape(mp, width), d_out.reshape(mp, width), w_out_b,
                           yp, tm=tm_p, tn=n_out)
            outs["pp"].append(pool_st)
            outs["cp"].append(conv_st)

            p = _norm_matmul(ys, norm_g[layer, 1], w_in_b, tm=ms, tn=1024)
            past = page_table.shape[1] * PAGE
            c_out, d_out, pool_st, conv_st = _odd_mix(
                p.reshape(bs, ts, 4 * width), state_pool[i], state_conv[i],
                cwg_b, c_scale[i], d_conv_w[i], tt=ts, pos0=past)
            ys = _out_proj(c_out.reshape(ms, width), d_out.reshape(ms, width), w_out_b,
                           ys, tm=ms, tn=n_out)
            outs["ps"].append(pool_st)
            outs["cs"].append(conv_st)
        yp, ys = ffn_pair(yp, ys, layer, 1)

    return (yp.reshape(bp, tp, d), ys.reshape(bs, ts, d),
            jnp.stack(outs["kp"]), jnp.stack(outs["vp"]),
            jnp.stack(outs["ks"]), jnp.stack(outs["vs"]), jnp.stack(outs["av"]),
            jnp.stack(outs["pp"]), jnp.stack(outs["ps"]),
            jnp.stack(outs["cp"]), jnp.stack(outs["cs"]))
```

```python
import functools

import jax
import jax.numpy as jnp
from jax import lax
from jax.experimental import pallas as pl
from jax.experimental.pallas import tpu as pltpu

F32 = jnp.float32
BF16 = jnp.bfloat16

EPS = 1e-6
HEAD_DIM = 128
N_HEADS = 8
SB_SCALE = HEAD_DIM ** -0.5
PAGE = 128
POOL_WINDOWS = (2, 4, 8, 16)
POOL_GROUP = 256
POOL_HIST = 15
CONV_HIST = 2
HIST_PAD = 16
VMEM_LIMIT = 56 * 1024 * 1024


def _params(sem):
    return pltpu.CompilerParams(dimension_semantics=sem, vmem_limit_bytes=VMEM_LIMIT)


def _rms(x, g):
    return x * lax.rsqrt(jnp.mean(x * x, axis=-1, keepdims=True) + EPS) * g


def _dot(a, b):
    return jnp.dot(a, b, preferred_element_type=F32)


def _ffn_kernel(x_ref, xs_ref, g_ref, wg_ref, wu_ref, wd_ref, o_ref, os_ref, h_ref, hs_ref):
    i = pl.program_id(0)
    j = pl.program_id(1)

    @pl.when(j == 0)
    def _():
        x = x_ref[...]
        h_ref[...] = _rms(x, g_ref[...]).astype(BF16)
        o_ref[...] = x

    wg = wg_ref[...].astype(BF16)
    wu = wu_ref[...].astype(BF16)
    wd = wd_ref[...].astype(BF16)

    def half_swiglu(h):
        gate = _dot(h, wg)
        up = _dot(h, wu)
        return 0.5 * _dot((gate * jax.nn.sigmoid(gate) * up).astype(BF16), wd)

    o_ref[...] += half_swiglu(h_ref[...])

    @pl.when(i == 0)
    def _():
        @pl.when(j == 0)
        def _():
            hs_ref[...] = _rms(xs_ref[...], g_ref[...]).astype(BF16)
            os_ref[...] = xs_ref[...]

        os_ref[...] += half_swiglu(hs_ref[...])


def _ffn(x, xs, g, wg, wu, wd, *, layer, half, tm, tf):
    m, d = x.shape
    n_small = xs.shape[0]
    f = wg.shape[-1]
    rows = lambda i, j: (i, 0)
    const = lambda i, j: (0, 0)
    return pl.pallas_call(
        _ffn_kernel,
        grid=(m // tm, f // tf),
        in_specs=[
            pl.BlockSpec((tm, d), rows, pipeline_mode=pl.Buffered(1)),
            pl.BlockSpec((n_small, d), const),
            pl.BlockSpec((1, d), const),
            pl.BlockSpec((None, None, d, tf), lambda i, j: (layer, half, 0, j)),
            pl.BlockSpec((None, None, d, tf), lambda i, j: (layer, half, 0, j)),
            pl.BlockSpec((None, None, tf, d), lambda i, j: (layer, half, j, 0)),
        ],
        out_specs=[pl.BlockSpec((tm, d), rows), pl.BlockSpec((n_small, d), const)],
        out_shape=[jax.ShapeDtypeStruct((m, d), F32), jax.ShapeDtypeStruct((n_small, d), F32)],
        scratch_shapes=[pltpu.VMEM((tm, d), BF16), pltpu.VMEM((n_small, d), BF16)],
        compiler_params=_params(("arbitrary", "arbitrary")),
        name="ffn",
    )(x, xs, g.reshape(1, d), wg, wu, wd)


def _norm_matmul_kernel(x_ref, g_ref, w_ref, o_ref, h_ref):
    @pl.when(pl.program_id(1) == 0)
    def _():
        h_ref[...] = _rms(x_ref[...], g_ref[...]).astype(BF16)

    o_ref[...] = _dot(h_ref[...], w_ref[...])


def _norm_matmul(x, g, w, *, tm, tn):
    m, d = x.shape
    n = w.shape[1]
    return pl.pallas_call(
        _norm_matmul_kernel,
        grid=(m // tm, n // tn),
        in_specs=[
            pl.BlockSpec((tm, d), lambda i, j: (i, 0)),
            pl.BlockSpec((1, d), lambda i, j: (0, 0)),
            pl.BlockSpec((d, tn), lambda i, j: (0, j)),
        ],
        out_specs=pl.BlockSpec((tm, tn), lambda i, j: (i, j)),
        out_shape=jax.ShapeDtypeStruct((m, n), F32),
        scratch_shapes=[pltpu.VMEM((tm, d), BF16)],
        compiler_params=_params(("parallel", "arbitrary")),
        name="norm_matmul",
    )(x, g.reshape(1, d), w)


def _head_norm(p, g):
    cols = []
    for h in range(N_HEADS):
        cols.append(_rms(p[:, h * HEAD_DIM:(h + 1) * HEAD_DIM], g))
    return jnp.concatenate(cols, axis=-1)


def _even_in_kernel(x_ref, g_ref, w_ref, qg_ref, kg_ref, avg_ref, wmix_ref, bmix_ref,
                    q_ref, k_ref, v_ref, gated_ref, avn_ref, h_ref, *, rows):
    sec = q_ref.shape[-1]
    h_ref[...] = _rms(x_ref[...], g_ref[...]).astype(BF16)

    def section(s):
        return _dot(h_ref[...], w_ref[:, s * sec:(s + 1) * sec])

    q_ref[...] = _head_norm(section(0), qg_ref[...]).astype(BF16)
    k_ref[...] = _head_norm(section(1), kg_ref[...])
    v_ref[...] = section(2)
    au = jax.nn.gelu(section(3))
    avn = _rms(jax.nn.gelu(section(4)), avg_ref[...])
    avn_ref[...] = avn
    avb = avn.astype(BF16)
    k_pad = wmix_ref.shape[2] - rows
    for c in range(avn.shape[0] // rows):
        rs = slice(c * rows, (c + 1) * rows)
        for grp in range(N_HEADS):
            cs = slice(grp * HEAD_DIM, (grp + 1) * HEAD_DIM)
            blk = avb[rs, cs]
            if k_pad:
                blk = jnp.concatenate([blk, jnp.zeros((k_pad, HEAD_DIM), BF16)], axis=0)
            mix = _dot(wmix_ref[grp], blk) + bmix_ref[:, cs]
            gated_ref[rs, cs] = (au[rs, cs] * mix).astype(BF16)


def _even_in(x, g, w, qg, kg, avg, wmix, bmix, *, tm):
    m, d = x.shape
    rows = wmix.shape[1]
    assert tm % rows == 0
    sec = N_HEADS * HEAD_DIM
    row_blk = lambda i: (i, 0)
    const = lambda i: (0, 0)
    once = dict(pipeline_mode=pl.Buffered(1))
    out_blk = pl.BlockSpec((tm, sec), row_blk)
    return pl.pallas_call(
        functools.partial(_even_in_kernel, rows=rows),
        grid=(m // tm,),
        in_specs=[
            pl.BlockSpec((tm, d), row_blk),
            pl.BlockSpec((1, d), const),
            pl.BlockSpec((d, 5 * sec), const, **once),
            pl.BlockSpec((1, HEAD_DIM), const),
            pl.BlockSpec((1, HEAD_DIM), const),
            pl.BlockSpec((1, sec), const),
            pl.BlockSpec(wmix.shape, lambda i: (0, 0, 0)),
            pl.BlockSpec((rows, sec), const),
        ],
        out_specs=[out_blk] * 5,
        out_shape=[
            jax.ShapeDtypeStruct((m, sec), BF16),
            jax.ShapeDtypeStruct((m, sec), F32),
            jax.ShapeDtypeStruct((m, sec), F32),
            jax.ShapeDtypeStruct((m, sec), BF16),
            jax.ShapeDtypeStruct((m, sec), F32),
        ],
        scratch_shapes=[pltpu.VMEM((tm, d), BF16)],
        compiler_params=_params(("parallel",)),
        name="even_in",
    )(x, g.reshape(1, d), w, qg.reshape(1, HEAD_DIM), kg.reshape(1, HEAD_DIM),
      avg.reshape(1, sec), wmix, bmix)


def _log_keep(z):
    return -(jnp.maximum(z, 0.0) + jnp.log(1.0 + jnp.exp(-jnp.abs(z))))


def _suffix_sum_weights():
    row = jnp.arange(2 * PAGE)[:, None] % PAGE
    col = jnp.arange(2 * PAGE)[None, :]
    return jnp.where(col < PAGE, row > col, True).astype(BF16)


def _sb_tiles(qs, kbs, vbs, biases, cum_w, runs, mask):
    rows = qs[0].shape[0]
    zs = [lax.dot_general(q, kb, (((1,), (1,)), ((), ())), preferred_element_type=F32)
          * SB_SCALE + bias for q, kb, bias in zip(qs, kbs, biases)]
    keeps = [_log_keep(z) for z in zs]
    parts = []
    for keep in keeps:
        lk = keep if mask is None else jnp.where(mask, keep, 0.0)
        hi = lk.astype(BF16)
        lo = (lk - hi.astype(F32)).astype(BF16)
        parts.append(jnp.concatenate([hi, lo], axis=1))
    cum = _dot(jnp.concatenate(parts, axis=0), cum_w)
    outs, new_runs, run = [], [], None
    for c, (z, keep, vb) in enumerate(zip(zs, keeps, vbs)):
        run = run if runs[c] is None else runs[c]
        cum_c = cum[c * rows:(c + 1) * rows]
        a = jnp.exp(z + keep + cum_c[:, :PAGE] + run)
        if mask is not None:
            a = jnp.where(mask, a, 0.0)
        outs.append(_dot(a.astype(BF16), vb))
        run = run + cum_c[:, PAGE:]
        new_runs.append(run)
    return outs, new_runs


def _sb_prompt_kernel(bias_ref, q_ref, k_ref, v_ref, cw_ref, o_ref, kb_ref, vb_ref, *, heads):
    hb = pl.program_id(1)
    i = pl.program_id(2)

    @pl.when(i == 0)
    def _():
        kb_ref[...] = k_ref[...].astype(BF16)
        vb_ref[...] = v_ref[...].astype(BF16)

    cum_w = cw_ref[...]
    cols = [slice(hh * HEAD_DIM, (hh + 1) * HEAD_DIM) for hh in range(heads)]
    qs = [q_ref[:, cs] for cs in cols]
    biases = [bias_ref[hb * heads + hh] for hh in range(heads)]

    def blocks(start, carry, mask):
        accs, runs = carry
        ds, runs = _sb_tiles(qs, [kb_ref[pl.ds(start, PAGE), cs] for cs in cols],
                             [vb_ref[pl.ds(start, PAGE), cs] for cs in cols],
                             biases, cum_w, runs, mask)
        return tuple(acc + d for acc, d in zip(accs, ds)), tuple(runs)

    zero = (jnp.zeros((PAGE, HEAD_DIM), F32),) * heads
    below_diag = (lax.broadcasted_iota(jnp.int32, (PAGE, PAGE), 1)
                  < lax.broadcasted_iota(jnp.int32, (PAGE, PAGE), 0))
    carry = blocks(pl.multiple_of(i * PAGE, PAGE), (zero, zero), below_diag)

    def body(it, carry):
        return blocks(pl.multiple_of((i - 1 - it) * PAGE, PAGE), carry, None)

    accs, _ = lax.fori_loop(0, i, body, carry)
    for acc, cs in zip(accs, cols):
        o_ref[:, cs] = acc.astype(BF16)


def _sb_prompt(q, k, v, bias, *, batch, heads):
    m, width = q.shape
    t = m // batch
    nq = t // PAGE
    hw = heads * HEAD_DIM
    qo_spec = pl.BlockSpec((PAGE, hw), lambda b, h, i: (b * nq + i, h))
    kv_spec = pl.BlockSpec((t, hw), lambda b, h, i: (b, h))
    return pl.pallas_call(
        functools.partial(_sb_prompt_kernel, heads=heads),
        grid=(batch, width // hw, nq),
        in_specs=[pl.BlockSpec(memory_space=pltpu.SMEM), qo_spec, kv_spec, kv_spec,
                  pl.BlockSpec((2 * PAGE, 2 * PAGE), lambda b, h, i: (0, 0))],
        out_specs=qo_spec,
        out_shape=jax.ShapeDtypeStruct((m, width), BF16),
        scratch_shapes=[pltpu.VMEM((t, hw), BF16), pltpu.VMEM((t, hw), BF16)],
        compiler_params=_params(("parallel", "parallel", "arbitrary")),
        name="sb_prompt",
    )(bias, q, k, v, _suffix_sum_weights())


def _sb_sample_kernel(pt_ref, q_ref, bias_ref, cw_ref, knew_ref, vnew_ref, *rest, tq, pages):
    kpage_refs, vpage_refs = rest[:pages], rest[pages:2 * pages]
    o_ref, acc_ref, run_ref = rest[2 * pages:]
    j = pl.program_id(1)
    q = q_ref[0]
    bias = bias_ref[...]
    cum_w = cw_ref[...]

    @pl.when(j == 0)
    def _():
        t_pos = lax.rem(lax.broadcasted_iota(jnp.int32, (q.shape[0], PAGE), 0), tq)
        s_pos = lax.broadcasted_iota(jnp.int32, (q.shape[0], PAGE), 1)
        ds, runs = _sb_tiles([q], [knew_ref[0].astype(BF16)], [vnew_ref[0].astype(BF16)], [bias],
                             cum_w, [jnp.zeros((q.shape[0], PAGE), F32)], s_pos < t_pos)
        acc_ref[...] = ds[0]
        run_ref[...] = runs[0]

    @pl.when(j > 0)
    def _():
        def page(ref):
            heads = [ref[0, pl.ds(h, PAGE, stride=N_HEADS), :] for h in range(N_HEADS)]
            return jnp.concatenate(heads, axis=1).astype(BF16)

        ds, runs = _sb_tiles([q] * pages, [page(r) for r in kpage_refs],
                             [page(r) for r in vpage_refs], [bias] * pages, cum_w,
                             [run_ref[...]] + [None] * (pages - 1), None)
        acc_ref[...] += functools.reduce(lambda x, y: x + y, ds)
        run_ref[...] = runs[-1]

    @pl.when(j == pl.num_programs(1) - 1)
    def _():
        for h in range(N_HEADS):
            cs = slice(h * HEAD_DIM, (h + 1) * HEAD_DIM)
            o_ref[0, :, cs] = acc_ref[h * tq:(h + 1) * tq, cs].astype(BF16)


def _sb_sample(q, k_new, v_new, cache_k, cache_v, page_table, bias, *, batch, page_base, pages):
    m, width = q.shape
    tq = m // batch
    n_pages = page_table.shape[1]
    assert n_pages % pages == 0
    rows = N_HEADS * tq
    q4 = q.reshape(batch, tq, N_HEADS, HEAD_DIM).transpose(0, 2, 1, 3)
    eye = jnp.eye(N_HEADS, dtype=BF16)
    q_rows = (q4[:, :, :, None, :] * eye[None, :, None, :, None]).reshape(batch, rows, width)
    bias_rows = jnp.broadcast_to(jnp.repeat(bias, tq)[:, None], (rows, PAGE))
    pad = ((0, 0), (0, PAGE - tq), (0, 0))
    k_pad = jnp.pad(k_new.reshape(batch, tq, width), pad)
    v_pad = jnp.pad(v_new.reshape(batch, tq, width), pad)

    def page_spec(p):
        def index(b, j, pt):
            return (page_base + pt[b, n_pages - 1 - (jnp.maximum(j, 1) - 1) * pages - p], 0, 0)
        return pl.BlockSpec((1, PAGE * N_HEADS, HEAD_DIM), index)

    per_b = lambda b, j, pt: (b, 0, 0)
    const = lambda b, j, pt: (0, 0)
    new_spec = pl.BlockSpec((1, PAGE, width), per_b)
    page_specs = [page_spec(p) for p in range(pages)]
    out = pl.pallas_call(
        functools.partial(_sb_sample_kernel, tq=tq, pages=pages),
        grid_spec=pltpu.PrefetchScalarGridSpec(
            num_scalar_prefetch=1,
            grid=(batch, n_pages // pages + 1),
            in_specs=[
                pl.BlockSpec((1, rows, width), per_b),
                pl.BlockSpec((rows, PAGE), const),
                pl.BlockSpec((2 * PAGE, 2 * PAGE), const),
                new_spec, new_spec, *page_specs, *page_specs,
            ],
            out_specs=pl.BlockSpec((1, tq, width), per_b),
            scratch_shapes=[pltpu.VMEM((rows, width), F32), pltpu.VMEM((rows, PAGE), F32)],
        ),
        out_shape=jax.ShapeDtypeStruct((batch, tq, width), BF16),
        compiler_params=_params(("parallel", "arbitrary")),
        name="sb_sample",
    )(page_table, q_rows, bias_rows, _suffix_sum_weights(), k_pad, v_pad,
      *([cache_k] * pages), *([cache_v] * pages))
    return out.reshape(m, width)


def _out_proj_kernel(a_ref, b_ref, wa_ref, wb_ref, r_ref, o_ref):
    o_ref[...] = r_ref[...] + _dot(a_ref[...], wa_ref[...]) + _dot(b_ref[...], wb_ref[...])


def _out_proj(a, b, w, resid, *, tm, tn):
    m, ka = a.shape
    n = w.shape[1]
    assert b.shape[1] == ka and w.shape[0] == 2 * ka
    return pl.pallas_call(
        _out_proj_kernel,
        grid=(m // tm, n // tn),
        in_specs=[
            pl.BlockSpec((tm, ka), lambda i, j: (i, 0)),
            pl.BlockSpec((tm, ka), lambda i, j: (i, 0)),
            pl.BlockSpec((ka, tn), lambda i, j: (0, j)),
            pl.BlockSpec((ka, tn), lambda i, j: (1, j)),
            pl.BlockSpec((tm, tn), lambda i, j: (i, j)),
        ],
        out_specs=pl.BlockSpec((tm, tn), lambda i, j: (i, j)),
        out_shape=jax.ShapeDtypeStruct((m, n), F32),
        compiler_params=_params(("parallel", "parallel")),
        name="out_proj",
    )(a, b, w, w, resid)


def _odd_mix_kernel(*refs, tt, pos0, project):
    n_in = 3 if project else 1
    (pool_prev_ref, conv_prev_ref, cwg_ref, cscale_ref, convw_ref,
     c_ref, d_ref, pool_out_ref, conv_out_ref, xbuf_ref, ubuf_ref) = refs[n_in:]
    t_idx = pl.program_id(1)
    width = c_ref.shape[-1]

    @pl.when(t_idx == 0)
    def _():
        xbuf_ref[HIST_PAD - POOL_HIST:HIST_PAD, :] = pool_prev_ref[0]
        ubuf_ref[HIST_PAD - CONV_HIST:HIST_PAD, :] = conv_prev_ref[0]

    if project:
        x_ref, g_ref, w_ref = refs[:n_in]
        h = _rms(x_ref[0], g_ref[...]).astype(BF16)
        section = lambda s: _dot(h, w_ref[:, s * width:(s + 1) * width])
    else:
        p_ref, = refs[:n_in]
        section = lambda s: p_ref[0, :, s * width:(s + 1) * width]
    xc = section(0)
    gb = section(1)
    u = section(2) * section(3)
    xbuf_ref[HIST_PAD:HIST_PAD + tt, :] = xc
    ubuf_ref[HIST_PAD:HIST_PAD + tt, :] = u

    seen = pos0 + t_idx * tt + lax.broadcasted_iota(jnp.int32, (tt, 1), 0) + 1
    for grp, win in enumerate(POOL_WINDOWS):
        cs = slice(grp * POOL_GROUP, (grp + 1) * POOL_GROUP)
        s = xc[:, cs]
        for back in range(1, win):
            s = s + xbuf_ref[HIST_PAD - back:HIST_PAD - back + tt, cs]
        count = jnp.minimum(seen, win).astype(F32)
        pooled = s / count - xc[:, cs]
        mixed = _dot(pooled.astype(BF16), cwg_ref[grp]) * cscale_ref[:, cs]
        c_ref[0, :, cs] = mixed.astype(BF16)

    conv = (ubuf_ref[HIST_PAD - 2:HIST_PAD - 2 + tt, :] * convw_ref[0:1, :]
            + ubuf_ref[HIST_PAD - 1:HIST_PAD - 1 + tt, :] * convw_ref[1:2, :]
            + u * convw_ref[2:3, :])
    d_ref[0] = (gb * conv).astype(BF16)

    pool_tail = xbuf_ref[HIST_PAD + tt - POOL_HIST:HIST_PAD + tt, :]
    conv_tail = ubuf_ref[HIST_PAD + tt - CONV_HIST:HIST_PAD + tt, :]
    pool_out_ref[0] = pool_tail
    conv_out_ref[0] = conv_tail
    xbuf_ref[HIST_PAD - POOL_HIST:HIST_PAD, :] = pool_tail
    ubuf_ref[HIST_PAD - CONV_HIST:HIST_PAD, :] = conv_tail


def _odd_mix(rows_in, pool_prev, conv_prev, cwg, cscale, convw, *, tt, pos0):
    batch, t, _ = rows_in[0].shape
    width = pool_prev.shape[-1]
    seq_blk = lambda b, i: (b, i, 0)
    per_b = lambda b, i: (b, 0, 0)
    const = lambda b, i: (0, 0)
    project = len(rows_in) == 3
    if project:
        x, g, w = rows_in
        d = x.shape[-1]
        rows_in = (x, g.reshape(1, d), w)
        rows_specs = [pl.BlockSpec((1, tt, d), seq_blk), pl.BlockSpec((1, d), const),
                      pl.BlockSpec(w.shape, const, pipeline_mode=pl.Buffered(1))]
    else:
        rows_specs = [pl.BlockSpec((1, tt, 4 * width), seq_blk)]
    return pl.pallas_call(
        functools.partial(_odd_mix_kernel, tt=tt, pos0=pos0, project=project),
        grid=(batch, t // tt),
        in_specs=rows_specs + [
            pl.BlockSpec((1, POOL_HIST, width), per_b),
            pl.BlockSpec((1, CONV_HIST, width), per_b),
            pl.BlockSpec(cwg.shape, lambda b, i: (0, 0, 0)),
            pl.BlockSpec((1, width), lambda b, i: (0, 0)),
            pl.BlockSpec(convw.shape, lambda b, i: (0, 0)),
        ],
        out_specs=[
            pl.BlockSpec((1, tt, width), seq_blk),
            pl.BlockSpec((1, tt, width), seq_blk),
            pl.BlockSpec((1, POOL_HIST, width), per_b),
            pl.BlockSpec((1, CONV_HIST, width), per_b),
        ],
        out_shape=[
            jax.ShapeDtypeStruct((batch, t, width), BF16),
            jax.ShapeDtypeStruct((batch, t, width), BF16),
            jax.ShapeDtypeStruct((batch, POOL_HIST, width), F32),
            jax.ShapeDtypeStruct((batch, CONV_HIST, width), F32),
        ],
        scratch_shapes=[pltpu.VMEM((HIST_PAD + tt, width), F32),
                        pltpu.VMEM((HIST_PAD + tt, width), F32)],
        compiler_params=_params(("parallel", "arbitrary")),
        name="odd_mix",
    )(*rows_in, pool_prev, conv_prev, cwg, cscale.reshape(1, width), convw)


def _pick_tile(m, target):
    return target if m % target == 0 else m


def kernel(x_prompt, x_sample, cache_k, cache_v, state_pool, state_conv, page_table, norm_g, ffn_w_gate, ffn_w_up, ffn_w_down, w_in_even, w_out_even, q_norm_g, k_norm_g, sb_bias, a_v_norm_g, a_ws, a_bs, w_in_odd, w_out_odd, c_wg, c_scale, d_conv_w):
    bp, tp, d = x_prompt.shape
    bs, ts, _ = x_sample.shape
    depth = norm_g.shape[0]
    mp, ms = bp * tp, bs * ts
    sec = N_HEADS * HEAD_DIM

    tm_p = _pick_tile(mp, 512)
    n_out = w_out_even.shape[-1]

    def ffn_pair(yp, ys, layer, half):
        return _ffn(yp, ys, norm_g[layer, 2 * half], ffn_w_gate, ffn_w_up, ffn_w_down,
                    layer=layer, half=half, tm=_pick_tile(mp, 1024),
                    tf=_pick_tile(ffn_w_gate.shape[-1], 256))

    yp = x_prompt.reshape(mp, d)
    ys = x_sample.reshape(ms, d)
    outs = {name: [] for name in ("kp", "vp", "ks", "vs", "av", "pp", "ps", "cp", "cs")}

    for layer in range(depth):
        i = layer // 2
        yp, ys = ffn_pair(yp, ys, layer, 0)
        if layer % 2 == 0:
            w_in_b, w_out_b = w_in_even[i].astype(BF16), w_out_even[i].astype(BF16)
            tril_p = jnp.tril(a_ws[i]).astype(BF16)
            bias_p = jnp.repeat(a_bs[i].T, HEAD_DIM, axis=1)
            tril_s = jnp.tril(a_ws[i][:, :ts, :ts])
            eye = jnp.eye(bs, dtype=F32)
            wmix_s = (eye[None, :, None, :, None] * tril_s[:, None, :, None, :]).reshape(
                N_HEADS, ms, ms).astype(BF16)
            wmix_s = jnp.pad(wmix_s, ((0, 0), (0, 0), (0, max(0, HEAD_DIM - ms))))
            bias_s = jnp.tile(jnp.repeat(a_bs[i][:, :ts].T, HEAD_DIM, axis=1), (bs, 1))

            q, k, v, gated, _ = _even_in(yp, norm_g[layer, 1], w_in_b, q_norm_g[i],
                                         k_norm_g[i], a_v_norm_g[i], tril_p, bias_p,
                                         tm=_pick_tile(mp, 256))
            sb = _sb_prompt(q, k, v, sb_bias[i], batch=bp, heads=8)
            yp = _out_proj(gated, sb, w_out_b, yp, tm=tm_p, tn=n_out)
            outs["kp"].append(k.reshape(bp, tp, N_HEADS, HEAD_DIM))
            outs["vp"].append(v.reshape(bp, tp, N_HEADS, HEAD_DIM))

            q, k, v, gated, avn = _even_in(ys, norm_g[layer, 1], w_in_b, q_norm_g[i],
                                           k_norm_g[i], a_v_norm_g[i], wmix_s, bias_s, tm=ms)
            n_phys = cache_k.shape[1]
            pool_shape = (cache_k.shape[0] * n_phys, PAGE * N_HEADS, HEAD_DIM)
            sb = _sb_sample(q, k, v, cache_k.reshape(pool_shape), cache_v.reshape(pool_shape),
                            page_table, sb_bias[i], batch=bs, page_base=i * n_phys, pages=8)
            ys = _out_proj(gated, sb, w_out_b, ys, tm=ms, tn=n_out)
            outs["ks"].append(k.reshape(bs, ts, N_HEADS, HEAD_DIM))
            outs["vs"].append(v.reshape(bs, ts, N_HEADS, HEAD_DIM))
            outs["av"].append(avn.reshape(bs, ts, sec))
        else:
            width = c_scale.shape[-1]
            w_in_b, w_out_b = w_in_odd[i].astype(BF16), w_out_odd[i].astype(BF16)
            cwg_b = c_wg[i].astype(BF16)
            c_out, d_out, pool_st, conv_st = _odd_mix(
                (yp.reshape(bp, tp, d), norm_g[layer, 1], w_in_b),
                jnp.zeros((bp, POOL_HIST, width), F32), jnp.zeros((bp, CONV_HIST, width), F32),
                cwg_b, c_scale[i], d_conv_w[i], tt=_pick_tile(tp, 256), pos0=0)
            yp = _out_proj(c_out.reshape(mp, width), d_out.reshape(mp, width), w_out_b,
                           yp, tm=tm_p, tn=n_out)
            outs["pp"].append(pool_st)
            outs["cp"].append(conv_st)

            p = _norm_matmul(ys, norm_g[layer, 1], w_in_b, tm=ms, tn=1024)
            past = page_table.shape[1] * PAGE
            c_out, d_out, pool_st, conv_st = _odd_mix(
                (p.reshape(bs, ts, 4 * width),), state_pool[i], state_conv[i],
                cwg_b, c_scale[i], d_conv_w[i], tt=ts, pos0=past)
            ys = _out_proj(c_out.reshape(ms, width), d_out.reshape(ms, width), w_out_b,
                           ys, tm=ms, tn=n_out)
            outs["ps"].append(pool_st)
            outs["cs"].append(conv_st)
        yp, ys = ffn_pair(yp, ys, layer, 1)

    return (yp.reshape(bp, tp, d), ys.reshape(bs, ts, d),
            jnp.stack(outs["kp"]), jnp.stack(outs["vp"]),
            jnp.stack(outs["ks"]), jnp.stack(outs["vs"]), jnp.stack(outs["av"]),
            jnp.stack(outs["pp"]), jnp.stack(outs["ps"]),
            jnp.stack(outs["cp"]), jnp.stack(outs["cs"]))
```

```python
import functools

import jax
import jax.numpy as jnp
from jax import lax
from jax.experimental import pallas as pl
from jax.experimental.pallas import tpu as pltpu

F32 = jnp.float32
BF16 = jnp.bfloat16

EPS = 1e-6
HEAD_DIM = 128
N_HEADS = 8
SB_SCALE = HEAD_DIM ** -0.5
PAGE = 128
POOL_WINDOWS = (2, 4, 8, 16)
POOL_GROUP = 256
POOL_HIST = 15
CONV_HIST = 2
HIST_PAD = 16
VMEM_LIMIT = 56 * 1024 * 1024


def _params(sem):
    return pltpu.CompilerParams(dimension_semantics=sem, vmem_limit_bytes=VMEM_LIMIT)


def _rms(x, g):
    return x * lax.rsqrt(jnp.mean(x * x, axis=-1, keepdims=True) + EPS) * g


def _dot(a, b):
    return jnp.dot(a, b, preferred_element_type=F32)


def _ffn_kernel(x_hbm, xs_ref, g_ref, wg_hbm, wu_hbm, wd_hbm, o_ref, os_ref,
                x_buf, h_ref, wg_buf, wu_buf, wd_buf, x_sem, w_sem, *, layer, half, tm, tf):
    i = pl.program_id(0)
    last = pl.num_programs(0) - 1
    n_tiles = wg_hbm.shape[-1] // tf
    n_small = xs_ref.shape[0]

    def x_copy(tile):
        return pltpu.make_async_copy(x_hbm.at[pl.ds(tile * tm, tm), :], x_buf, x_sem.at[0])

    def w_copies(j, slot):
        cols = pl.ds(j * tf, tf)
        return (
            pltpu.make_async_copy(wg_hbm.at[layer, half, :, cols], wg_buf.at[slot], w_sem.at[0, slot]),
            pltpu.make_async_copy(wu_hbm.at[layer, half, :, cols], wu_buf.at[slot], w_sem.at[1, slot]),
            pltpu.make_async_copy(wd_hbm.at[layer, half, cols, :], wd_buf.at[slot], w_sem.at[2, slot]),
        )

    @pl.when(i == 0)
    def _():
        x_copy(0).start()
        for c in w_copies(0, 0):
            c.start()
        xs = xs_ref[...]
        h_ref[tm:tm + n_small, :] = _rms(xs, g_ref[...]).astype(BF16)
        os_ref[...] = xs

    x_copy(i).wait()
    x = x_buf[...]
    h_ref[0:tm, :] = _rms(x, g_ref[...]).astype(BF16)
    o_ref[...] = x
    x_copy(jnp.minimum(i + 1, last)).start()

    def walk(rows):
        def two_tiles(jj, carry):
            for slot in (0, 1):
                j = 2 * jj + slot
                for c in w_copies(j, slot):
                    c.wait()
                for c in w_copies(lax.rem(j + 1, n_tiles), 1 - slot):
                    c.start()
                h = h_ref[0:rows, :]
                gate = _dot(h, wg_buf[slot].astype(BF16))
                up = _dot(h, wu_buf[slot].astype(BF16))
                act = (gate * jax.nn.sigmoid(gate) * up).astype(BF16)
                d = 0.5 * _dot(act, wd_buf[slot].astype(BF16))
                o_ref[...] += d[0:tm]
                if rows > tm:
                    os_ref[...] += d[tm:rows]
            return carry

        lax.fori_loop(0, n_tiles // 2, two_tiles, 0)

    @pl.when(i == 0)
    def _():
        walk(tm + n_small)

    @pl.when(i > 0)
    def _():
        walk(tm)

    @pl.when(i == last)
    def _():
        x_copy(last).wait()
        for c in w_copies(0, 0):
            c.wait()


def _ffn(x, xs, g, wg, wu, wd, *, layer, half, tm, tf):
    m, d = x.shape
    n_small = xs.shape[0]
    f = wg.shape[-1]
    assert m % tm == 0 and f % (2 * tf) == 0
    rows = lambda i: (i, 0)
    const = lambda i: (0, 0)
    hbm = pl.BlockSpec(memory_space=pl.ANY)
    return pl.pallas_call(
        functools.partial(_ffn_kernel, layer=layer, half=half, tm=tm, tf=tf),
        grid=(m // tm,),
        in_specs=[hbm, pl.BlockSpec((n_small, d), const), pl.BlockSpec((1, d), const),
                  hbm, hbm, hbm],
        out_specs=[pl.BlockSpec((tm, d), rows), pl.BlockSpec((n_small, d), const)],
        out_shape=[jax.ShapeDtypeStruct((m, d), F32), jax.ShapeDtypeStruct((n_small, d), F32)],
        scratch_shapes=[
            pltpu.VMEM((tm, d), F32),
            pltpu.VMEM((tm + n_small, d), BF16),
            pltpu.VMEM((2, d, tf), F32),
            pltpu.VMEM((2, d, tf), F32),
            pltpu.VMEM((2, tf, d), F32),
            pltpu.SemaphoreType.DMA((1,)),
            pltpu.SemaphoreType.DMA((3, 2)),
        ],
        compiler_params=_params(("arbitrary",)),
        name="ffn",
    )(x, xs, g.reshape(1, d), wg, wu, wd)


def _norm_matmul_kernel(x_ref, g_ref, w_ref, o_ref, h_ref):
    @pl.when(pl.program_id(1) == 0)
    def _():
        h_ref[...] = _rms(x_ref[...], g_ref[...]).astype(BF16)

    o_ref[...] = _dot(h_ref[...], w_ref[...])


def _norm_matmul(x, g, w, *, tm, tn):
    m, d = x.shape
    n = w.shape[1]
    return pl.pallas_call(
        _norm_matmul_kernel,
        grid=(m // tm, n // tn),
        in_specs=[
            pl.BlockSpec((tm, d), lambda i, j: (i, 0)),
            pl.BlockSpec((1, d), lambda i, j: (0, 0)),
            pl.BlockSpec((d, tn), lambda i, j: (0, j)),
        ],
        out_specs=pl.BlockSpec((tm, tn), lambda i, j: (i, j)),
        out_shape=jax.ShapeDtypeStruct((m, n), F32),
        scratch_shapes=[pltpu.VMEM((tm, d), BF16)],
        compiler_params=_params(("parallel", "arbitrary")),
        name="norm_matmul",
    )(x, g.reshape(1, d), w)


def _head_norm(p, g):
    cols = []
    for h in range(N_HEADS):
        cols.append(_rms(p[:, h * HEAD_DIM:(h + 1) * HEAD_DIM], g))
    return jnp.concatenate(cols, axis=-1)


def _even_in_kernel(x_ref, g_ref, w_ref, qg_ref, kg_ref, avg_ref, wmix_ref, bmix_ref,
                    q_ref, k_ref, v_ref, gated_ref, avn_ref, h_ref, *, rows):
    sec = q_ref.shape[-1]
    h_ref[...] = _rms(x_ref[...], g_ref[...]).astype(BF16)

    def section(s):
        return _dot(h_ref[...], w_ref[:, s * sec:(s + 1) * sec])

    q_ref[...] = _head_norm(section(0), qg_ref[...]).astype(BF16)
    k_ref[...] = _head_norm(section(1), kg_ref[...])
    v_ref[...] = section(2)
    au = jax.nn.gelu(section(3))
    avn = _rms(jax.nn.gelu(section(4)), avg_ref[...])
    avn_ref[...] = avn
    avb = avn.astype(BF16)
    k_pad = wmix_ref.shape[2] - rows
    for c in range(avn.shape[0] // rows):
        rs = slice(c * rows, (c + 1) * rows)
        for grp in range(N_HEADS):
            cs = slice(grp * HEAD_DIM, (grp + 1) * HEAD_DIM)
            blk = avb[rs, cs]
            if k_pad:
                blk = jnp.concatenate([blk, jnp.zeros((k_pad, HEAD_DIM), BF16)], axis=0)
            mix = _dot(wmix_ref[grp], blk) + bmix_ref[:, cs]
            gated_ref[rs, cs] = (au[rs, cs] * mix).astype(BF16)


def _even_in(x, g, w, qg, kg, avg, wmix, bmix, *, tm):
    m, d = x.shape
    rows = wmix.shape[1]
    assert tm % rows == 0
    sec = N_HEADS * HEAD_DIM
    row_blk = lambda i: (i, 0)
    const = lambda i: (0, 0)
    once = dict(pipeline_mode=pl.Buffered(1))
    out_blk = pl.BlockSpec((tm, sec), row_blk)
    return pl.pallas_call(
        functools.partial(_even_in_kernel, rows=rows),
        grid=(m // tm,),
        in_specs=[
            pl.BlockSpec((tm, d), row_blk),
            pl.BlockSpec((1, d), const),
            pl.BlockSpec((d, 5 * sec), const, **once),
            pl.BlockSpec((1, HEAD_DIM), const),
            pl.BlockSpec((1, HEAD_DIM), const),
            pl.BlockSpec((1, sec), const),
            pl.BlockSpec(wmix.shape, lambda i: (0, 0, 0)),
            pl.BlockSpec((rows, sec), const),
        ],
        out_specs=[out_blk] * 5,
        out_shape=[
            jax.ShapeDtypeStruct((m, sec), BF16),
            jax.ShapeDtypeStruct((m, sec), F32),
            jax.ShapeDtypeStruct((m, sec), F32),
            jax.ShapeDtypeStruct((m, sec), BF16),
            jax.ShapeDtypeStruct((m, sec), F32),
        ],
        scratch_shapes=[pltpu.VMEM((tm, d), BF16)],
        compiler_params=_params(("parallel",)),
        name="even_in",
    )(x, g.reshape(1, d), w, qg.reshape(1, HEAD_DIM), kg.reshape(1, HEAD_DIM),
      avg.reshape(1, sec), wmix, bmix)


def _softplus(z):
    return jnp.maximum(z, 0.0) + jnp.log(1.0 + jnp.exp(-jnp.abs(z)))


def _suffix_sum_weights():
    row = jnp.arange(2 * PAGE)[:, None] % PAGE
    col = jnp.arange(2 * PAGE)[None, :]
    return -jnp.where(col < PAGE, row > col, True).astype(BF16)


def _sb_tiles(qs, kbs, vbs, biases, cum_w, runs, mask):
    rows = qs[0].shape[0]
    zs = [lax.dot_general(q, kb, (((1,), (1,)), ((), ())), preferred_element_type=F32)
          * SB_SCALE + bias for q, kb, bias in zip(qs, kbs, biases)]
    sps = [_softplus(z) for z in zs]
    parts = []
    for sp in sps:
        sp = sp if mask is None else jnp.where(mask, sp, 0.0)
        hi = sp.astype(BF16)
        lo = (sp - hi.astype(F32)).astype(BF16)
        parts.append(jnp.concatenate([hi, lo], axis=1))
    cum = _dot(jnp.concatenate(parts, axis=0), cum_w)
    outs, new_runs, run = [], [], None
    for c, (z, sp, vb) in enumerate(zip(zs, sps, vbs)):
        run = run if runs[c] is None else runs[c]
        cum_c = cum[c * rows:(c + 1) * rows]
        a = jnp.exp(z - sp + cum_c[:, :PAGE] + run)
        if mask is not None:
            a = jnp.where(mask, a, 0.0)
        outs.append(_dot(a.astype(BF16), vb))
        run = run + cum_c[:, PAGE:]
        new_runs.append(run)
    return outs, new_runs


def _sb_prompt_kernel(bias_ref, q_ref, k_ref, v_ref, cw_ref, o_ref, kb_ref, vb_ref, *, heads):
    hb = pl.program_id(1)
    i = pl.program_id(2)

    @pl.when(i == 0)
    def _():
        kb_ref[...] = k_ref[...].astype(BF16)
        vb_ref[...] = v_ref[...].astype(BF16)

    cum_w = cw_ref[...]
    cols = [slice(hh * HEAD_DIM, (hh + 1) * HEAD_DIM) for hh in range(heads)]
    qs = [q_ref[:, cs] for cs in cols]
    biases = [bias_ref[hb * heads + hh] for hh in range(heads)]

    def blocks(start, carry, mask):
        accs, runs = carry
        ds, runs = _sb_tiles(qs, [kb_ref[pl.ds(start, PAGE), cs] for cs in cols],
                             [vb_ref[pl.ds(start, PAGE), cs] for cs in cols],
                             biases, cum_w, runs, mask)
        return tuple(acc + d for acc, d in zip(accs, ds)), tuple(runs)

    zero = (jnp.zeros((PAGE, HEAD_DIM), F32),) * heads
    below_diag = (lax.broadcasted_iota(jnp.int32, (PAGE, PAGE), 1)
                  < lax.broadcasted_iota(jnp.int32, (PAGE, PAGE), 0))
    carry = blocks(pl.multiple_of(i * PAGE, PAGE), (zero, zero), below_diag)

    def body(it, carry):
        return blocks(pl.multiple_of((i - 1 - it) * PAGE, PAGE), carry, None)

    accs, _ = lax.fori_loop(0, i, body, carry)
    for acc, cs in zip(accs, cols):
        o_ref[:, cs] = acc.astype(BF16)


def _sb_prompt(q, k, v, bias, *, batch, heads):
    m, width = q.shape
    t = m // batch
    nq = t // PAGE
    hw = heads * HEAD_DIM
    qo_spec = pl.BlockSpec((PAGE, hw), lambda b, h, i: (b * nq + i, h))
    kv_spec = pl.BlockSpec((t, hw), lambda b, h, i: (b, h))
    return pl.pallas_call(
        functools.partial(_sb_prompt_kernel, heads=heads),
        grid=(batch, width // hw, nq),
        in_specs=[pl.BlockSpec(memory_space=pltpu.SMEM), qo_spec, kv_spec, kv_spec,
                  pl.BlockSpec((2 * PAGE, 2 * PAGE), lambda b, h, i: (0, 0))],
        out_specs=qo_spec,
        out_shape=jax.ShapeDtypeStruct((m, width), BF16),
        scratch_shapes=[pltpu.VMEM((t, hw), BF16), pltpu.VMEM((t, hw), BF16)],
        compiler_params=_params(("parallel", "parallel", "arbitrary")),
        name="sb_prompt",
    )(bias, q, k, v, _suffix_sum_weights())


def _sb_sample_kernel(pt_ref, q_ref, bias_ref, cw_ref, knew_ref, vnew_ref, *rest, tq, pages):
    kpage_refs, vpage_refs = rest[:pages], rest[pages:2 * pages]
    o_ref, acc_ref, run_ref = rest[2 * pages:]
    j = pl.program_id(1)
    q = q_ref[0]
    bias = bias_ref[...]
    cum_w = cw_ref[...]

    @pl.when(j == 0)
    def _():
        t_pos = lax.rem(lax.broadcasted_iota(jnp.int32, (q.shape[0], PAGE), 0), tq)
        s_pos = lax.broadcasted_iota(jnp.int32, (q.shape[0], PAGE), 1)
        ds, runs = _sb_tiles([q], [knew_ref[0].astype(BF16)], [vnew_ref[0].astype(BF16)], [bias],
                             cum_w, [jnp.zeros((q.shape[0], PAGE), F32)], s_pos < t_pos)
        acc_ref[...] = ds[0]
        run_ref[...] = runs[0]

    @pl.when(j > 0)
    def _():
        def page(ref):
            heads = [ref[0, pl.ds(h, PAGE, stride=N_HEADS), :] for h in range(N_HEADS)]
            return jnp.concatenate(heads, axis=1).astype(BF16)

        ds, runs = _sb_tiles([q] * pages, [page(r) for r in kpage_refs],
                             [page(r) for r in vpage_refs], [bias] * pages, cum_w,
                             [run_ref[...]] + [None] * (pages - 1), None)
        acc_ref[...] += functools.reduce(lambda x, y: x + y, ds)
        run_ref[...] = runs[-1]

    @pl.when(j == pl.num_programs(1) - 1)
    def _():
        for h in range(N_HEADS):
            cs = slice(h * HEAD_DIM, (h + 1) * HEAD_DIM)
            o_ref[0, :, cs] = acc_ref[h * tq:(h + 1) * tq, cs].astype(BF16)


def _sb_sample(q, k_new, v_new, cache_k, cache_v, page_table, bias, *, batch, page_base, pages):
    m, width = q.shape
    tq = m // batch
    n_pages = page_table.shape[1]
    assert n_pages % pages == 0
    rows = N_HEADS * tq
    q4 = q.reshape(batch, tq, N_HEADS, HEAD_DIM).transpose(0, 2, 1, 3)
    eye = jnp.eye(N_HEADS, dtype=BF16)
    q_rows = (q4[:, :, :, None, :] * eye[None, :, None, :, None]).reshape(batch, rows, width)
    bias_rows = jnp.broadcast_to(jnp.repeat(bias, tq)[:, None], (rows, PAGE))
    pad = ((0, 0), (0, PAGE - tq), (0, 0))
    k_pad = jnp.pad(k_new.reshape(batch, tq, width), pad)
    v_pad = jnp.pad(v_new.reshape(batch, tq, width), pad)

    def page_spec(p):
        def index(b, j, pt):
            return (page_base + pt[b, n_pages - 1 - (jnp.maximum(j, 1) - 1) * pages - p], 0, 0)
        return pl.BlockSpec((1, PAGE * N_HEADS, HEAD_DIM), index)

    per_b = lambda b, j, pt: (b, 0, 0)
    const = lambda b, j, pt: (0, 0)
    new_spec = pl.BlockSpec((1, PAGE, width), per_b)
    page_specs = [page_spec(p) for p in range(pages)]
    out = pl.pallas_call(
        functools.partial(_sb_sample_kernel, tq=tq, pages=pages),
        grid_spec=pltpu.PrefetchScalarGridSpec(
            num_scalar_prefetch=1,
            grid=(batch, n_pages // pages + 1),
            in_specs=[
                pl.BlockSpec((1, rows, width), per_b),
                pl.BlockSpec((rows, PAGE), const),
                pl.BlockSpec((2 * PAGE, 2 * PAGE), const),
                new_spec, new_spec, *page_specs, *page_specs,
            ],
            out_specs=pl.BlockSpec((1, tq, width), per_b),
            scratch_shapes=[pltpu.VMEM((rows, width), F32), pltpu.VMEM((rows, PAGE), F32)],
        ),
        out_shape=jax.ShapeDtypeStruct((batch, tq, width), BF16),
        compiler_params=_params(("parallel", "arbitrary")),
        name="sb_sample",
    )(page_table, q_rows, bias_rows, _suffix_sum_weights(), k_pad, v_pad,
      *([cache_k] * pages), *([cache_v] * pages))
    return out.reshape(m, width)


def _out_proj_kernel(a_ref, b_ref, wa_ref, wb_ref, r_ref, o_ref):
    o_ref[...] = r_ref[...] + _dot(a_ref[...], wa_ref[...]) + _dot(b_ref[...], wb_ref[...])


def _out_proj(a, b, w, resid, *, tm, tn):
    m, ka = a.shape
    n = w.shape[1]
    assert b.shape[1] == ka and w.shape[0] == 2 * ka
    return pl.pallas_call(
        _out_proj_kernel,
        grid=(m // tm, n // tn),
        in_specs=[
            pl.BlockSpec((tm, ka), lambda i, j: (i, 0)),
            pl.BlockSpec((tm, ka), lambda i, j: (i, 0)),
            pl.BlockSpec((ka, tn), lambda i, j: (0, j)),
            pl.BlockSpec((ka, tn), lambda i, j: (1, j)),
            pl.BlockSpec((tm, tn), lambda i, j: (i, j)),
        ],
        out_specs=pl.BlockSpec((tm, tn), lambda i, j: (i, j)),
        out_shape=jax.ShapeDtypeStruct((m, n), F32),
        compiler_params=_params(("parallel", "parallel")),
        name="out_proj",
    )(a, b, w, w, resid)


def _odd_mix_kernel(*refs, tt, pos0, project):
    n_in = 3 if project else 1
    (pool_prev_ref, conv_prev_ref, cwg_ref, cscale_ref, convw_ref,
     c_ref, d_ref, pool_out_ref, conv_out_ref, xbuf_ref, ubuf_ref) = refs[n_in:]
    t_idx = pl.program_id(1)
    width = c_ref.shape[-1]

    @pl.when(t_idx == 0)
    def _():
        xbuf_ref[HIST_PAD - POOL_HIST:HIST_PAD, :] = pool_prev_ref[0]
        ubuf_ref[HIST_PAD - CONV_HIST:HIST_PAD, :] = conv_prev_ref[0]

    if project:
        x_ref, g_ref, w_ref = refs[:n_in]
        h = _rms(x_ref[0], g_ref[...]).astype(BF16)
        section = lambda s: _dot(h, w_ref[:, s * width:(s + 1) * width])
    else:
        p_ref, = refs[:n_in]
        section = lambda s: p_ref[0, :, s * width:(s + 1) * width]
    xc = section(0)
    gb = section(1)
    u = section(2) * section(3)
    xbuf_ref[HIST_PAD:HIST_PAD + tt, :] = xc
    ubuf_ref[HIST_PAD:HIST_PAD + tt, :] = u

    seen = pos0 + t_idx * tt + lax.broadcasted_iota(jnp.int32, (tt, 1), 0) + 1
    for grp, win in enumerate(POOL_WINDOWS):
        cs = slice(grp * POOL_GROUP, (grp + 1) * POOL_GROUP)
        s = xc[:, cs]
        for back in range(1, win):
            s = s + xbuf_ref[HIST_PAD - back:HIST_PAD - back + tt, cs]
        count = jnp.minimum(seen, win).astype(F32)
        pooled = s / count - xc[:, cs]
        mixed = _dot(pooled.astype(BF16), cwg_ref[grp]) * cscale_ref[:, cs]
        c_ref[0, :, cs] = mixed.astype(BF16)

    conv = (ubuf_ref[HIST_PAD - 2:HIST_PAD - 2 + tt, :] * convw_ref[0:1, :]
            + ubuf_ref[HIST_PAD - 1:HIST_PAD - 1 + tt, :] * convw_ref[1:2, :]
            + u * convw_ref[2:3, :])
    d_ref[0] = (gb * conv).astype(BF16)

    pool_tail = xbuf_ref[HIST_PAD + tt - POOL_HIST:HIST_PAD + tt, :]
    conv_tail = ubuf_ref[HIST_PAD + tt - CONV_HIST:HIST_PAD + tt, :]
    pool_out_ref[0] = pool_tail
    conv_out_ref[0] = conv_tail
    xbuf_ref[HIST_PAD - POOL_HIST:HIST_PAD, :] = pool_tail
    ubuf_ref[HIST_PAD - CONV_HIST:HIST_PAD, :] = conv_tail


def _odd_mix(rows_in, pool_prev, conv_prev, cwg, cscale, convw, *, tt, pos0):
    batch, t, _ = rows_in[0].shape
    width = pool_prev.shape[-1]
    seq_blk = lambda b, i: (b, i, 0)
    per_b = lambda b, i: (b, 0, 0)
    const = lambda b, i: (0, 0)
    project = len(rows_in) == 3
    if project:
        x, g, w = rows_in
        d = x.shape[-1]
        rows_in = (x, g.reshape(1, d), w)
        rows_specs = [pl.BlockSpec((1, tt, d), seq_blk), pl.BlockSpec((1, d), const),
                      pl.BlockSpec(w.shape, const, pipeline_mode=pl.Buffered(1))]
    else:
        rows_specs = [pl.BlockSpec((1, tt, 4 * width), seq_blk)]
    return pl.pallas_call(
        functools.partial(_odd_mix_kernel, tt=tt, pos0=pos0, project=project),
        grid=(batch, t // tt),
        in_specs=rows_specs + [
            pl.BlockSpec((1, POOL_HIST, width), per_b),
            pl.BlockSpec((1, CONV_HIST, width), per_b),
            pl.BlockSpec(cwg.shape, lambda b, i: (0, 0, 0)),
            pl.BlockSpec((1, width), lambda b, i: (0, 0)),
            pl.BlockSpec(convw.shape, lambda b, i: (0, 0)),
        ],
        out_specs=[
            pl.BlockSpec((1, tt, width), seq_blk),
            pl.BlockSpec((1, tt, width), seq_blk),
            pl.BlockSpec((1, POOL_HIST, width), per_b),
            pl.BlockSpec((1, CONV_HIST, width), per_b),
        ],
        out_shape=[
            jax.ShapeDtypeStruct((batch, t, width), BF16),
            jax.ShapeDtypeStruct((batch, t, width), BF16),
            jax.ShapeDtypeStruct((batch, POOL_HIST, width), F32),
            jax.ShapeDtypeStruct((batch, CONV_HIST, width), F32),
        ],
        scratch_shapes=[pltpu.VMEM((HIST_PAD + tt, width), F32),
                        pltpu.VMEM((HIST_PAD + tt, width), F32)],
        compiler_params=_params(("parallel", "arbitrary")),
        name="odd_mix",
    )(*rows_in, pool_prev, conv_prev, cwg, cscale.reshape(1, width), convw)


def _pick_tile(m, target):
    return target if m % target == 0 else m


def kernel(x_prompt, x_sample, cache_k, cache_v, state_pool, state_conv, page_table, norm_g, ffn_w_gate, ffn_w_up, ffn_w_down, w_in_even, w_out_even, q_norm_g, k_norm_g, sb_bias, a_v_norm_g, a_ws, a_bs, w_in_odd, w_out_odd, c_wg, c_scale, d_conv_w):
    bp, tp, d = x_prompt.shape
    bs, ts, _ = x_sample.shape
    depth = norm_g.shape[0]
    mp, ms = bp * tp, bs * ts
    sec = N_HEADS * HEAD_DIM

    tm_p = _pick_tile(mp, 512)
    n_out = w_out_even.shape[-1]

    def ffn_pair(yp, ys, layer, half):
        return _ffn(yp, ys, norm_g[layer, 2 * half], ffn_w_gate, ffn_w_up, ffn_w_down,
                    layer=layer, half=half, tm=_pick_tile(mp, 1024),
                    tf=_pick_tile(ffn_w_gate.shape[-1], 256))

    yp = x_prompt.reshape(mp, d)
    ys = x_sample.reshape(ms, d)
    outs = {name: [] for name in ("kp", "vp", "ks", "vs", "av", "pp", "ps", "cp", "cs")}

    for layer in range(depth):
        i = layer // 2
        yp, ys = ffn_pair(yp, ys, layer, 0)
        if layer % 2 == 0:
            w_in_b, w_out_b = w_in_even[i].astype(BF16), w_out_even[i].astype(BF16)
            tril_p = jnp.tril(a_ws[i]).astype(BF16)
            bias_p = jnp.repeat(a_bs[i].T, HEAD_DIM, axis=1)
            tril_s = jnp.tril(a_ws[i][:, :ts, :ts])
            eye = jnp.eye(bs, dtype=F32)
            wmix_s = (eye[None, :, None, :, None] * tril_s[:, None, :, None, :]).reshape(
                N_HEADS, ms, ms).astype(BF16)
            wmix_s = jnp.pad(wmix_s, ((0, 0), (0, 0), (0, max(0, HEAD_DIM - ms))))
            bias_s = jnp.tile(jnp.repeat(a_bs[i][:, :ts].T, HEAD_DIM, axis=1), (bs, 1))

            q, k, v, gated, _ = _even_in(yp, norm_g[layer, 1], w_in_b, q_norm_g[i],
                                         k_norm_g[i], a_v_norm_g[i], tril_p, bias_p,
                                         tm=_pick_tile(mp, 256))
            sb = _sb_prompt(q, k, v, sb_bias[i], batch=bp, heads=8)
            yp = _out_proj(gated, sb, w_out_b, yp, tm=tm_p, tn=n_out)
            outs["kp"].append(k.reshape(bp, tp, N_HEADS, HEAD_DIM))
            outs["vp"].append(v.reshape(bp, tp, N_HEADS, HEAD_DIM))

            q, k, v, gated, avn = _even_in(ys, norm_g[layer, 1], w_in_b, q_norm_g[i],
                                           k_norm_g[i], a_v_norm_g[i], wmix_s, bias_s, tm=ms)
            n_phys = cache_k.shape[1]
            pool_shape = (cache_k.shape[0] * n_phys, PAGE * N_HEADS, HEAD_DIM)
            sb = _sb_sample(q, k, v, cache_k.reshape(pool_shape), cache_v.reshape(pool_shape),
                            page_table, sb_bias[i], batch=bs, page_base=i * n_phys, pages=16)
            ys = _out_proj(gated, sb, w_out_b, ys, tm=ms, tn=n_out)
            outs["ks"].append(k.reshape(bs, ts, N_HEADS, HEAD_DIM))
            outs["vs"].append(v.reshape(bs, ts, N_HEADS, HEAD_DIM))
            outs["av"].append(avn.reshape(bs, ts, sec))
        else:
            width = c_scale.shape[-1]
            w_in_b, w_out_b = w_in_odd[i].astype(BF16), w_out_odd[i].astype(BF16)
            cwg_b = c_wg[i].astype(BF16)
            c_out, d_out, pool_st, conv_st = _odd_mix(
                (yp.reshape(bp, tp, d), norm_g[layer, 1], w_in_b),
                jnp.zeros((bp, POOL_HIST, width), F32), jnp.zeros((bp, CONV_HIST, width), F32),
                cwg_b, c_scale[i], d_conv_w[i], tt=_pick_tile(tp, 256), pos0=0)
            yp = _out_proj(c_out.reshape(mp, width), d_out.reshape(mp, width), w_out_b,
                           yp, tm=tm_p, tn=n_out)
            outs["pp"].append(pool_st)
            outs["cp"].append(conv_st)

            p = _norm_matmul(ys, norm_g[layer, 1], w_in_b, tm=ms, tn=1024)
            past = page_table.shape[1] * PAGE
            c_out, d_out, pool_st, conv_st = _odd_mix(
                (p.reshape(bs, ts, 4 * width),), state_pool[i], state_conv[i],
                cwg_b, c_scale[i], d_conv_w[i], tt=ts, pos0=past)
            ys = _out_proj(c_out.reshape(ms, width), d_out.reshape(ms, width), w_out_b,
                           ys, tm=ms, tn=n_out)
            outs["ps"].append(pool_st)
            outs["cs"].append(conv_st)
        yp, ys = ffn_pair(yp, ys, layer, 1)

    return (yp.reshape(bp, tp, d), ys.reshape(bs, ts, d),
            jnp.stack(outs["kp"]), jnp.stack(outs["vp"]),
            jnp.stack(outs["ks"]), jnp.stack(outs["vs"]), jnp.stack(outs["av"]),
            jnp.stack(outs["pp"]), jnp.stack(outs["ps"]),
            jnp.stack(outs["cp"]), jnp.stack(outs["cs"]))
```

```python
import functools

import jax
import jax.numpy as jnp
from jax import lax
from jax.experimental import pallas as pl
from jax.experimental.pallas import tpu as pltpu

F32 = jnp.float32
BF16 = jnp.bfloat16

EPS = 1e-6
HEAD_DIM = 128
N_HEADS = 8
SB_SCALE = HEAD_DIM ** -0.5
PAGE = 128
POOL_WINDOWS = (2, 4, 8, 16)
POOL_GROUP = 256
POOL_HIST = 15
CONV_HIST = 2
HIST_PAD = 16
VMEM_LIMIT = 56 * 1024 * 1024
W_SLOTS = 3
W_AHEAD = W_SLOTS - 1


def _params(sem):
    return pltpu.CompilerParams(dimension_semantics=sem, vmem_limit_bytes=VMEM_LIMIT)


def _rms(x, g):
    return x * lax.rsqrt(jnp.mean(x * x, axis=-1, keepdims=True) + EPS) * g


def _dot(a, b):
    return jnp.dot(a, b, preferred_element_type=F32)


def _ffn_kernel(x_hbm, xs_ref, g_ref, wg_hbm, wu_hbm, wd_hbm, o_ref, os_ref,
                x_buf, h_ref, wg_buf, wu_buf, wd_buf, x_sem, w_sem, *, layer, half, tm, tf):
    i = pl.program_id(0)
    last = pl.num_programs(0) - 1
    n_tiles = wg_hbm.shape[-1] // tf
    n_small = xs_ref.shape[0]

    def x_copy(tile):
        return pltpu.make_async_copy(x_hbm.at[pl.ds(tile * tm, tm), :], x_buf, x_sem.at[0])

    def w_copies(j, slot):
        cols = pl.ds(j * tf, tf)
        return (
            pltpu.make_async_copy(wg_hbm.at[layer, half, :, cols], wg_buf.at[slot], w_sem.at[0, slot]),
            pltpu.make_async_copy(wu_hbm.at[layer, half, :, cols], wu_buf.at[slot], w_sem.at[1, slot]),
            pltpu.make_async_copy(wd_hbm.at[layer, half, cols, :], wd_buf.at[slot], w_sem.at[2, slot]),
        )

    @pl.when(i == 0)
    def _():
        x_copy(0).start()
        for ahead in range(W_AHEAD):
            for c in w_copies(ahead, ahead):
                c.start()
        xs = xs_ref[...]
        h_ref[tm:tm + n_small, :] = _rms(xs, g_ref[...]).astype(BF16)
        os_ref[...] = xs

    x_copy(i).wait()
    x = x_buf[...]
    h_ref[0:tm, :] = _rms(x, g_ref[...]).astype(BF16)
    o_ref[...] = x
    x_copy(jnp.minimum(i + 1, last)).start()

    def walk(rows):
        def two_tiles(jj, carry):
            for u in (0, 1):
                j = 2 * jj + u
                step = i * n_tiles + j
                slot = lax.rem(step, W_SLOTS)
                for c in w_copies(j, slot):
                    c.wait()
                for c in w_copies(lax.rem(j + W_AHEAD, n_tiles), lax.rem(step + W_AHEAD, W_SLOTS)):
                    c.start()
                h = h_ref[0:rows, :]
                gate = _dot(h, wg_buf[slot].astype(BF16))
                up = _dot(h, wu_buf[slot].astype(BF16))
                act = (gate * jax.nn.sigmoid(gate) * up).astype(BF16)
                d = 0.5 * _dot(act, wd_buf[slot].astype(BF16))
                o_ref[...] += d[0:tm]
                if rows > tm:
                    os_ref[...] += d[tm:rows]
            return carry

        lax.fori_loop(0, n_tiles // 2, two_tiles, 0)

    @pl.when(i == 0)
    def _():
        walk(tm + n_small)

    @pl.when(i > 0)
    def _():
        walk(tm)

    @pl.when(i == last)
    def _():
        x_copy(last).wait()
        for ahead in range(W_AHEAD):
            for c in w_copies(ahead, lax.rem((last + 1) * n_tiles + ahead, W_SLOTS)):
                c.wait()


def _ffn(x, xs, g, wg, wu, wd, *, layer, half, tm, tf):
    m, d = x.shape
    n_small = xs.shape[0]
    f = wg.shape[-1]
    assert m % tm == 0 and f % (2 * tf) == 0
    rows = lambda i: (i, 0)
    const = lambda i: (0, 0)
    hbm = pl.BlockSpec(memory_space=pl.ANY)
    return pl.pallas_call(
        functools.partial(_ffn_kernel, layer=layer, half=half, tm=tm, tf=tf),
        grid=(m // tm,),
        in_specs=[hbm, pl.BlockSpec((n_small, d), const), pl.BlockSpec((1, d), const),
                  hbm, hbm, hbm],
        out_specs=[pl.BlockSpec((tm, d), rows), pl.BlockSpec((n_small, d), const)],
        out_shape=[jax.ShapeDtypeStruct((m, d), F32), jax.ShapeDtypeStruct((n_small, d), F32)],
        scratch_shapes=[
            pltpu.VMEM((tm, d), F32),
            pltpu.VMEM((tm + n_small, d), BF16),
            pltpu.VMEM((W_SLOTS, d, tf), F32),
            pltpu.VMEM((W_SLOTS, d, tf), F32),
            pltpu.VMEM((W_SLOTS, tf, d), F32),
            pltpu.SemaphoreType.DMA((1,)),
            pltpu.SemaphoreType.DMA((3, W_SLOTS)),
        ],
        compiler_params=_params(("arbitrary",)),
        name="ffn",
    )(x, xs, g.reshape(1, d), wg, wu, wd)


def _norm_matmul_kernel(x_ref, g_ref, w_ref, o_ref, h_ref):
    @pl.when(pl.program_id(1) == 0)
    def _():
        h_ref[...] = _rms(x_ref[...], g_ref[...]).astype(BF16)

    o_ref[...] = _dot(h_ref[...], w_ref[...])


def _norm_matmul(x, g, w, *, tm, tn):
    m, d = x.shape
    n = w.shape[1]
    return pl.pallas_call(
        _norm_matmul_kernel,
        grid=(m // tm, n // tn),
        in_specs=[
            pl.BlockSpec((tm, d), lambda i, j: (i, 0)),
            pl.BlockSpec((1, d), lambda i, j: (0, 0)),
            pl.BlockSpec((d, tn), lambda i, j: (0, j)),
        ],
        out_specs=pl.BlockSpec((tm, tn), lambda i, j: (i, j)),
        out_shape=jax.ShapeDtypeStruct((m, n), F32),
        scratch_shapes=[pltpu.VMEM((tm, d), BF16)],
        compiler_params=_params(("parallel", "arbitrary")),
        name="norm_matmul",
    )(x, g.reshape(1, d), w)


def _head_norm(p, g):
    cols = []
    for h in range(N_HEADS):
        cols.append(_rms(p[:, h * HEAD_DIM:(h + 1) * HEAD_DIM], g))
    return jnp.concatenate(cols, axis=-1)


def _even_in_kernel(x_ref, g_ref, w_ref, qg_ref, kg_ref, avg_ref, wmix_ref, bmix_ref,
                    q_ref, k_ref, v_ref, gated_ref, avn_ref, h_ref, *, rows):
    sec = q_ref.shape[-1]
    h_ref[...] = _rms(x_ref[...], g_ref[...]).astype(BF16)

    def section(s):
        return _dot(h_ref[...], w_ref[:, s * sec:(s + 1) * sec])

    q_ref[...] = _head_norm(section(0), qg_ref[...]).astype(BF16)
    k_ref[...] = _head_norm(section(1), kg_ref[...])
    v_ref[...] = section(2)
    au = jax.nn.gelu(section(3))
    avn = _rms(jax.nn.gelu(section(4)), avg_ref[...])
    avn_ref[...] = avn
    avb = avn.astype(BF16)
    k_pad = wmix_ref.shape[2] - rows
    for c in range(avn.shape[0] // rows):
        rs = slice(c * rows, (c + 1) * rows)
        for grp in range(N_HEADS):
            cs = slice(grp * HEAD_DIM, (grp + 1) * HEAD_DIM)
            blk = avb[rs, cs]
            if k_pad:
                blk = jnp.concatenate([blk, jnp.zeros((k_pad, HEAD_DIM), BF16)], axis=0)
            mix = _dot(wmix_ref[grp], blk) + bmix_ref[:, cs]
            gated_ref[rs, cs] = (au[rs, cs] * mix).astype(BF16)


def _even_in(x, g, w, qg, kg, avg, wmix, bmix, *, tm):
    m, d = x.shape
    rows = wmix.shape[1]
    assert tm % rows == 0
    sec = N_HEADS * HEAD_DIM
    row_blk = lambda i: (i, 0)
    const = lambda i: (0, 0)
    once = dict(pipeline_mode=pl.Buffered(1))
    out_blk = pl.BlockSpec((tm, sec), row_blk)
    return pl.pallas_call(
        functools.partial(_even_in_kernel, rows=rows),
        grid=(m // tm,),
        in_specs=[
            pl.BlockSpec((tm, d), row_blk),
            pl.BlockSpec((1, d), const),
            pl.BlockSpec((d, 5 * sec), const, **once),
            pl.BlockSpec((1, HEAD_DIM), const),
            pl.BlockSpec((1, HEAD_DIM), const),
            pl.BlockSpec((1, sec), const),
            pl.BlockSpec(wmix.shape, lambda i: (0, 0, 0)),
            pl.BlockSpec((rows, sec), const),
        ],
        out_specs=[out_blk] * 5,
        out_shape=[
            jax.ShapeDtypeStruct((m, sec), BF16),
            jax.ShapeDtypeStruct((m, sec), F32),
            jax.ShapeDtypeStruct((m, sec), F32),
            jax.ShapeDtypeStruct((m, sec), BF16),
            jax.ShapeDtypeStruct((m, sec), F32),
        ],
        scratch_shapes=[pltpu.VMEM((tm, d), BF16)],
        compiler_params=_params(("parallel",)),
        name="even_in",
    )(x, g.reshape(1, d), w, qg.reshape(1, HEAD_DIM), kg.reshape(1, HEAD_DIM),
      avg.reshape(1, sec), wmix, bmix)


def _softplus(z):
    return jnp.maximum(z, 0.0) + jnp.log(1.0 + jnp.exp(-jnp.abs(z)))


def _suffix_sum_weights():
    row = jnp.arange(2 * PAGE)[:, None] % PAGE
    col = jnp.arange(2 * PAGE)[None, :]
    return -jnp.where(col < PAGE, row > col, True).astype(BF16)


def _sb_logits(qs, kbs, biases):
    return [lax.dot_general(q, kb, (((1,), (1,)), ((), ())), preferred_element_type=F32)
            * SB_SCALE + bias for q, kb, bias in zip(qs, kbs, biases)]


def _sb_weights(zs, cum_w, runs, mask, side_work=()):
    rows = zs[0].shape[0]
    sps, parts = [], []
    for c, z in enumerate(zs):
        sp = _softplus(z)
        sps.append(sp)
        sp = sp if mask is None else jnp.where(mask, sp, 0.0)
        hi = sp.astype(BF16)
        lo = (sp - hi.astype(F32)).astype(BF16)
        parts.append(jnp.concatenate([hi, lo], axis=1))
        if side_work:
            side_work[c]()
    cum = _dot(jnp.concatenate(parts, axis=0), cum_w)
    weights, new_runs, run = [], [], None
    for c, (z, sp) in enumerate(zip(zs, sps)):
        run = run if runs[c] is None else runs[c]
        cum_c = cum[c * rows:(c + 1) * rows]
        a = jnp.exp(z - sp + cum_c[:, :PAGE] + run)
        if mask is not None:
            a = jnp.where(mask, a, 0.0)
        weights.append(a.astype(BF16))
        run = run + cum_c[:, PAGE:]
        new_runs.append(run)
    return weights, new_runs


def _sb_prompt_kernel(bias_ref, q_ref, k_ref, v_ref, cw_ref, o_ref,
                      kb_ref, vb_ref, acc_ref, run_ref, *, heads):
    hb = pl.program_id(1)
    i = pl.program_id(2)

    @pl.when(i == 0)
    def _():
        kb_ref[...] = k_ref[...].astype(BF16)
        vb_ref[...] = v_ref[...].astype(BF16)

    cum_w = cw_ref[...]
    cols = [slice(hh * HEAD_DIM, (hh + 1) * HEAD_DIM) for hh in range(heads)]
    qs = [q_ref[:, cs] for cs in cols]
    biases = [bias_ref[hb * heads + hh] for hh in range(heads)]

    def logits(block):
        start = pl.multiple_of(block * PAGE, PAGE)
        return _sb_logits(qs, [kb_ref[pl.ds(start, PAGE), cs] for cs in cols], biases)

    def weights(zs, mask, side_work=()):
        a, runs = _sb_weights(zs, cum_w, [run_ref[:, cs] for cs in cols], mask, side_work)
        for run, cs in zip(runs, cols):
            run_ref[:, cs] = run
        return a

    def accumulate(a, block):
        start = pl.multiple_of(block * PAGE, PAGE)
        for a_c, cs in zip(a, cols):
            acc_ref[:, cs] += _dot(a_c, vb_ref[pl.ds(start, PAGE), cs])

    acc_ref[...] = jnp.zeros_like(acc_ref)
    run_ref[...] = jnp.zeros_like(run_ref)
    below_diag = (lax.broadcasted_iota(jnp.int32, (PAGE, PAGE), 1)
                  < lax.broadcasted_iota(jnp.int32, (PAGE, PAGE), 0))
    a_diag = weights(logits(i), below_diag)

    def body(it, carry):
        zs, a_prev = carry
        block = i - 1 - it
        nxt = pl.multiple_of(jnp.maximum(block - 1, 0) * PAGE, PAGE)
        prev = pl.multiple_of((block + 1) * PAGE, PAGE)
        zs_next = [None] * heads

        def side(c):
            def work():
                cs = cols[c]
                zs_next[c], = _sb_logits([qs[c]], [kb_ref[pl.ds(nxt, PAGE), cs]], [biases[c]])
                acc_ref[:, cs] += _dot(a_prev[c], vb_ref[pl.ds(prev, PAGE), cs])
            return work

        a_cur = weights(zs, None, [side(c) for c in range(heads)])
        return tuple(zs_next), tuple(a_cur)

    _, a_last = lax.fori_loop(0, i, body, (tuple(logits(jnp.maximum(i - 1, 0))), tuple(a_diag)))
    accumulate(a_last, 0)
    o_ref[...] = acc_ref[...].astype(BF16)


def _sb_prompt(q, k, v, bias, *, batch, heads):
    m, width = q.shape
    t = m // batch
    nq = t // PAGE
    hw = heads * HEAD_DIM
    qo_spec = pl.BlockSpec((PAGE, hw), lambda b, h, i: (b * nq + i, h))
    kv_spec = pl.BlockSpec((t, hw), lambda b, h, i: (b, h))
    return pl.pallas_call(
        functools.partial(_sb_prompt_kernel, heads=heads),
        grid=(batch, width // hw, nq),
        in_specs=[pl.BlockSpec(memory_space=pltpu.SMEM), qo_spec, kv_spec, kv_spec,
                  pl.BlockSpec((2 * PAGE, 2 * PAGE), lambda b, h, i: (0, 0))],
        out_specs=qo_spec,
        out_shape=jax.ShapeDtypeStruct((m, width), BF16),
        scratch_shapes=[pltpu.VMEM((t, hw), BF16), pltpu.VMEM((t, hw), BF16),
                        pltpu.VMEM((PAGE, hw), F32), pltpu.VMEM((PAGE, hw), F32)],
        compiler_params=_params(("parallel", "parallel", "arbitrary")),
        name="sb_prompt",
    )(bias, q, k, v, _suffix_sum_weights())


def _sb_sample_kernel(pt_ref, q_ref, bias_ref, cw_ref, knew_ref, vnew_ref, *rest, tq, pages):
    kpage_refs, vpage_refs = rest[:pages], rest[pages:2 * pages]
    o_ref, acc_ref, run_ref = rest[2 * pages:]
    j = pl.program_id(1)
    q = q_ref[0]
    bias = bias_ref[...]
    cum_w = cw_ref[...]

    @pl.when(j == 0)
    def _():
        t_pos = lax.rem(lax.broadcasted_iota(jnp.int32, (q.shape[0], PAGE), 0), tq)
        s_pos = lax.broadcasted_iota(jnp.int32, (q.shape[0], PAGE), 1)
        zs = _sb_logits([q], [knew_ref[0].astype(BF16)], [bias])
        a, runs = _sb_weights(zs, cum_w, [jnp.zeros((q.shape[0], PAGE), F32)], s_pos < t_pos)
        acc_ref[...] = _dot(a[0], vnew_ref[0].astype(BF16))
        run_ref[...] = runs[0]

    @pl.when(j > 0)
    def _():
        def page(ref):
            heads = [ref[0, pl.ds(h, PAGE, stride=N_HEADS), :] for h in range(N_HEADS)]
            return jnp.concatenate(heads, axis=1).astype(BF16)

        zs = _sb_logits([q] * pages, [page(r) for r in kpage_refs], [bias] * pages)
        a, runs = _sb_weights(zs, cum_w, [run_ref[...]] + [None] * (pages - 1), None)
        ds = [_dot(a_p, page(r)) for a_p, r in zip(a, vpage_refs)]
        acc_ref[...] += functools.reduce(lambda x, y: x + y, ds)
        run_ref[...] = runs[-1]

    @pl.when(j == pl.num_programs(1) - 1)
    def _():
        for h in range(N_HEADS):
            cs = slice(h * HEAD_DIM, (h + 1) * HEAD_DIM)
            o_ref[0, :, cs] = acc_ref[h * tq:(h + 1) * tq, cs].astype(BF16)


def _sb_sample(q, k_new, v_new, cache_k, cache_v, page_table, bias, *, batch, page_base, pages):
    m, width = q.shape
    tq = m // batch
    n_pages = page_table.shape[1]
    assert n_pages % pages == 0
    rows = N_HEADS * tq
    q4 = q.reshape(batch, tq, N_HEADS, HEAD_DIM).transpose(0, 2, 1, 3)
    eye = jnp.eye(N_HEADS, dtype=BF16)
    q_rows = (q4[:, :, :, None, :] * eye[None, :, None, :, None]).reshape(batch, rows, width)
    bias_rows = jnp.broadcast_to(jnp.repeat(bias, tq)[:, None], (rows, PAGE))
    pad = ((0, 0), (0, PAGE - tq), (0, 0))
    k_pad = jnp.pad(k_new.reshape(batch, tq, width), pad)
    v_pad = jnp.pad(v_new.reshape(batch, tq, width), pad)

    def page_spec(p):
        def index(b, j, pt):
            return (page_base + pt[b, n_pages - 1 - (jnp.maximum(j, 1) - 1) * pages - p], 0, 0)
        return pl.BlockSpec((1, PAGE * N_HEADS, HEAD_DIM), index)

    per_b = lambda b, j, pt: (b, 0, 0)
    const = lambda b, j, pt: (0, 0)
    new_spec = pl.BlockSpec((1, PAGE, width), per_b)
    page_specs = [page_spec(p) for p in range(pages)]
    out = pl.pallas_call(
        functools.partial(_sb_sample_kernel, tq=tq, pages=pages),
        grid_spec=pltpu.PrefetchScalarGridSpec(
            num_scalar_prefetch=1,
            grid=(batch, n_pages // pages + 1),
            in_specs=[
                pl.BlockSpec((1, rows, width), per_b),
                pl.BlockSpec((rows, PAGE), const),
                pl.BlockSpec((2 * PAGE, 2 * PAGE), const),
                new_spec, new_spec, *page_specs, *page_specs,
            ],
            out_specs=pl.BlockSpec((1, tq, width), per_b),
            scratch_shapes=[pltpu.VMEM((rows, width), F32), pltpu.VMEM((rows, PAGE), F32)],
        ),
        out_shape=jax.ShapeDtypeStruct((batch, tq, width), BF16),
        compiler_params=_params(("parallel", "arbitrary")),
        name="sb_sample",
    )(page_table, q_rows, bias_rows, _suffix_sum_weights(), k_pad, v_pad,
      *([cache_k] * pages), *([cache_v] * pages))
    return out.reshape(m, width)


def _out_proj_kernel(a_ref, b_ref, wa_ref, wb_ref, r_ref, o_ref):
    o_ref[...] = r_ref[...] + _dot(a_ref[...], wa_ref[...]) + _dot(b_ref[...], wb_ref[...])


def _out_proj(a, b, w, resid, *, tm, tn):
    m, ka = a.shape
    n = w.shape[1]
    assert b.shape[1] == ka and w.shape[0] == 2 * ka
    return pl.pallas_call(
        _out_proj_kernel,
        grid=(m // tm, n // tn),
        in_specs=[
            pl.BlockSpec((tm, ka), lambda i, j: (i, 0)),
            pl.BlockSpec((tm, ka), lambda i, j: (i, 0)),
            pl.BlockSpec((ka, tn), lambda i, j: (0, j)),
            pl.BlockSpec((ka, tn), lambda i, j: (1, j)),
            pl.BlockSpec((tm, tn), lambda i, j: (i, j)),
        ],
        out_specs=pl.BlockSpec((tm, tn), lambda i, j: (i, j)),
        out_shape=jax.ShapeDtypeStruct((m, n), F32),
        compiler_params=_params(("parallel", "parallel")),
        name="out_proj",
    )(a, b, w, w, resid)


def _odd_mix_kernel(*refs, tt, pos0, project):
    n_in = 3 if project else 1
    (pool_prev_ref, conv_prev_ref, cwg_ref, cscale_ref, convw_ref,
     c_ref, d_ref, pool_out_ref, conv_out_ref, xbuf_ref, ubuf_ref) = refs[n_in:]
    t_idx = pl.program_id(1)
    width = c_ref.shape[-1]

    @pl.when(t_idx == 0)
    def _():
        xbuf_ref[HIST_PAD - POOL_HIST:HIST_PAD, :] = pool_prev_ref[0]
        ubuf_ref[HIST_PAD - CONV_HIST:HIST_PAD, :] = conv_prev_ref[0]

    if project:
        x_ref, g_ref, w_ref = refs[:n_in]
        h = _rms(x_ref[0], g_ref[...]).astype(BF16)
        section = lambda s: _dot(h, w_ref[:, s * width:(s + 1) * width])
    else:
        p_ref, = refs[:n_in]
        section = lambda s: p_ref[0, :, s * width:(s + 1) * width]
    xc = section(0)
    gb = section(1)
    u = section(2) * section(3)
    xbuf_ref[HIST_PAD:HIST_PAD + tt, :] = xc
    ubuf_ref[HIST_PAD:HIST_PAD + tt, :] = u

    seen = pos0 + t_idx * tt + lax.broadcasted_iota(jnp.int32, (tt, 1), 0) + 1
    for grp, win in enumerate(POOL_WINDOWS):
        cs = slice(grp * POOL_GROUP, (grp + 1) * POOL_GROUP)
        s = xc[:, cs]
        for back in range(1, win):
            s = s + xbuf_ref[HIST_PAD - back:HIST_PAD - back + tt, cs]
        count = jnp.minimum(seen, win).astype(F32)
        pooled = s / count - xc[:, cs]
        mixed = _dot(pooled.astype(BF16), cwg_ref[grp]) * cscale_ref[:, cs]
        c_ref[0, :, cs] = mixed.astype(BF16)

    conv = (ubuf_ref[HIST_PAD - 2:HIST_PAD - 2 + tt, :] * convw_ref[0:1, :]
            + ubuf_ref[HIST_PAD - 1:HIST_PAD - 1 + tt, :] * convw_ref[1:2, :]
            + u * convw_ref[2:3, :])
    d_ref[0] = (gb * conv).astype(BF16)

    pool_tail = xbuf_ref[HIST_PAD + tt - POOL_HIST:HIST_PAD + tt, :]
    conv_tail = ubuf_ref[HIST_PAD + tt - CONV_HIST:HIST_PAD + tt, :]
    pool_out_ref[0] = pool_tail
    conv_out_ref[0] = conv_tail
    xbuf_ref[HIST_PAD - POOL_HIST:HIST_PAD, :] = pool_tail
    ubuf_ref[HIST_PAD - CONV_HIST:HIST_PAD, :] = conv_tail


def _odd_mix(rows_in, pool_prev, conv_prev, cwg, cscale, convw, *, tt, pos0):
    batch, t, _ = rows_in[0].shape
    width = pool_prev.shape[-1]
    seq_blk = lambda b, i: (b, i, 0)
    per_b = lambda b, i: (b, 0, 0)
    const = lambda b, i: (0, 0)
    project = len(rows_in) == 3
    if project:
        x, g, w = rows_in
        d = x.shape[-1]
        rows_in = (x, g.reshape(1, d), w)
        rows_specs = [pl.BlockSpec((1, tt, d), seq_blk), pl.BlockSpec((1, d), const),
                      pl.BlockSpec(w.shape, const, pipeline_mode=pl.Buffered(1))]
    else:
        rows_specs = [pl.BlockSpec((1, tt, 4 * width), seq_blk)]
    return pl.pallas_call(
        functools.partial(_odd_mix_kernel, tt=tt, pos0=pos0, project=project),
        grid=(batch, t // tt),
        in_specs=rows_specs + [
            pl.BlockSpec((1, POOL_HIST, width), per_b),
            pl.BlockSpec((1, CONV_HIST, width), per_b),
            pl.BlockSpec(cwg.shape, lambda b, i: (0, 0, 0)),
            pl.BlockSpec((1, width), lambda b, i: (0, 0)),
            pl.BlockSpec(convw.shape, lambda b, i: (0, 0)),
        ],
        out_specs=[
            pl.BlockSpec((1, tt, width), seq_blk),
            pl.BlockSpec((1, tt, width), seq_blk),
            pl.BlockSpec((1, POOL_HIST, width), per_b),
            pl.BlockSpec((1, CONV_HIST, width), per_b),
        ],
        out_shape=[
            jax.ShapeDtypeStruct((batch, t, width), BF16),
            jax.ShapeDtypeStruct((batch, t, width), BF16),
            jax.ShapeDtypeStruct((batch, POOL_HIST, width), F32),
            jax.ShapeDtypeStruct((batch, CONV_HIST, width), F32),
        ],
        scratch_shapes=[pltpu.VMEM((HIST_PAD + tt, width), F32),
                        pltpu.VMEM((HIST_PAD + tt, width), F32)],
        compiler_params=_params(("parallel", "arbitrary")),
        name="odd_mix",
    )(*rows_in, pool_prev, conv_prev, cwg, cscale.reshape(1, width), convw)


def _pick_tile(m, target):
    return target if m % target == 0 else m


def kernel(x_prompt, x_sample, cache_k, cache_v, state_pool, state_conv, page_table, norm_g, ffn_w_gate, ffn_w_up, ffn_w_down, w_in_even, w_out_even, q_norm_g, k_norm_g, sb_bias, a_v_norm_g, a_ws, a_bs, w_in_odd, w_out_odd, c_wg, c_scale, d_conv_w):
    bp, tp, d = x_prompt.shape
    bs, ts, _ = x_sample.shape
    depth = norm_g.shape[0]
    mp, ms = bp * tp, bs * ts
    sec = N_HEADS * HEAD_DIM

    tm_p = _pick_tile(mp, 512)
    n_out = w_out_even.shape[-1]

    def ffn_pair(yp, ys, layer, half):
        return _ffn(yp, ys, norm_g[layer, 2 * half], ffn_w_gate, ffn_w_up, ffn_w_down,
                    layer=layer, half=half, tm=_pick_tile(mp, 1024),
                    tf=_pick_tile(ffn_w_gate.shape[-1], 256))

    yp = x_prompt.reshape(mp, d)
    ys = x_sample.reshape(ms, d)
    outs = {name: [] for name in ("kp", "vp", "ks", "vs", "av", "pp", "ps", "cp", "cs")}

    for layer in range(depth):
        i = layer // 2
        yp, ys = ffn_pair(yp, ys, layer, 0)
        if layer % 2 == 0:
            w_in_b, w_out_b = w_in_even[i].astype(BF16), w_out_even[i].astype(BF16)
            tril_p = jnp.tril(a_ws[i]).astype(BF16)
            bias_p = jnp.repeat(a_bs[i].T, HEAD_DIM, axis=1)
            tril_s = jnp.tril(a_ws[i][:, :ts, :ts])
            eye = jnp.eye(bs, dtype=F32)
            wmix_s = (eye[None, :, None, :, None] * tril_s[:, None, :, None, :]).reshape(
                N_HEADS, ms, ms).astype(BF16)
            wmix_s = jnp.pad(wmix_s, ((0, 0), (0, 0), (0, max(0, HEAD_DIM - ms))))
            bias_s = jnp.tile(jnp.repeat(a_bs[i][:, :ts].T, HEAD_DIM, axis=1), (bs, 1))

            q, k, v, gated, _ = _even_in(yp, norm_g[layer, 1], w_in_b, q_norm_g[i],
                                         k_norm_g[i], a_v_norm_g[i], tril_p, bias_p,
                                         tm=_pick_tile(mp, 256))
            sb = _sb_prompt(q, k, v, sb_bias[i], batch=bp, heads=8)
            yp = _out_proj(gated, sb, w_out_b, yp, tm=tm_p, tn=n_out)
            outs["kp"].append(k.reshape(bp, tp, N_HEADS, HEAD_DIM))
            outs["vp"].append(v.reshape(bp, tp, N_HEADS, HEAD_DIM))

            q, k, v, gated, avn = _even_in(ys, norm_g[layer, 1], w_in_b, q_norm_g[i],
                                           k_norm_g[i], a_v_norm_g[i], wmix_s, bias_s, tm=ms)
            n_phys = cache_k.shape[1]
            pool_shape = (cache_k.shape[0] * n_phys, PAGE * N_HEADS, HEAD_DIM)
            sb = _sb_sample(q, k, v, cache_k.reshape(pool_shape), cache_v.reshape(pool_shape),
                            page_table, sb_bias[i], batch=bs, page_base=i * n_phys, pages=16)
            ys = _out_proj(gated, sb, w_out_b, ys, tm=ms, tn=n_out)
            outs["ks"].append(k.reshape(bs, ts, N_HEADS, HEAD_DIM))
            outs["vs"].append(v.reshape(bs, ts, N_HEADS, HEAD_DIM))
            outs["av"].append(avn.reshape(bs, ts, sec))
        else:
            width = c_scale.shape[-1]
            w_in_b, w_out_b = w_in_odd[i].astype(BF16), w_out_odd[i].astype(BF16)
            cwg_b = c_wg[i].astype(BF16)
            c_out, d_out, pool_st, conv_st = _odd_mix(
                (yp.reshape(bp, tp, d), norm_g[layer, 1], w_in_b),
                jnp.zeros((bp, POOL_HIST, width), F32), jnp.zeros((bp, CONV_HIST, width), F32),
                cwg_b, c_scale[i], d_conv_w[i], tt=_pick_tile(tp, 256), pos0=0)
            yp = _out_proj(c_out.reshape(mp, width), d_out.reshape(mp, width), w_out_b,
                           yp, tm=tm_p, tn=n_out)
            outs["pp"].append(pool_st)
            outs["cp"].append(conv_st)

            p = _norm_matmul(ys, norm_g[layer, 1], w_in_b, tm=ms, tn=1024)
            past = page_table.shape[1] * PAGE
            c_out, d_out, pool_st, conv_st = _odd_mix(
                (p.reshape(bs, ts, 4 * width),), state_pool[i], state_conv[i],
                cwg_b, c_scale[i], d_conv_w[i], tt=ts, pos0=past)
            ys = _out_proj(c_out.reshape(ms, width), d_out.reshape(ms, width), w_out_b,
                           ys, tm=ms, tn=n_out)
            outs["ps"].append(pool_st)
            outs["cs"].append(conv_st)
        yp, ys = ffn_pair(yp, ys, layer, 1)

    return (yp.reshape(bp, tp, d), ys.reshape(bs, ts, d),
            jnp.stack(outs["kp"]), jnp.stack(outs["vp"]),
            jnp.stack(outs["ks"]), jnp.stack(outs["vs"]), jnp.stack(outs["av"]),
            jnp.stack(outs["pp"]), jnp.stack(outs["ps"]),
            jnp.stack(outs["cp"]), jnp.stack(outs["cs"]))
```

```python
import functools

import jax
import jax.numpy as jnp
from jax import lax
from jax.experimental import pallas as pl
from jax.experimental.pallas import tpu as pltpu

F32 = jnp.float32
BF16 = jnp.bfloat16

EPS = 1e-6
HEAD_DIM = 128
N_HEADS = 8
SB_SCALE = HEAD_DIM ** -0.5
PAGE = 128
POOL_WINDOWS = (2, 4, 8, 16)
POOL_GROUP = 256
POOL_HIST = 15
CONV_HIST = 2
HIST_PAD = 16
BF16_ROWS = 16
VMEM_LIMIT = 56 * 1024 * 1024
W_SLOTS = 3
W_AHEAD = W_SLOTS - 1


def _params(sem):
    return pltpu.CompilerParams(dimension_semantics=sem, vmem_limit_bytes=VMEM_LIMIT)


def _rms(x, g):
    return x * lax.rsqrt(jnp.mean(x * x, axis=-1, keepdims=True) + EPS) * g


def _dot(a, b):
    return jnp.dot(a, b, preferred_element_type=F32)


def _ffn_kernel(x_hbm, xs_ref, g_ref, wg_hbm, wu_hbm, wd_hbm, o_ref, os_ref,
                x_buf, h_ref, wg_buf, wu_buf, wd_buf, x_sem, w_sem, *, layer, half, tm, tf):
    i = pl.program_id(0)
    last = pl.num_programs(0) - 1
    n_tiles = wg_hbm.shape[-1] // tf
    n_small = xs_ref.shape[0]

    def x_copy(tile):
        return pltpu.make_async_copy(x_hbm.at[pl.ds(tile * tm, tm), :], x_buf, x_sem.at[0])

    def w_copies(j, slot):
        cols = pl.ds(j * tf, tf)
        return (
            pltpu.make_async_copy(wg_hbm.at[layer, half, :, cols], wg_buf.at[slot], w_sem.at[0, slot]),
            pltpu.make_async_copy(wu_hbm.at[layer, half, :, cols], wu_buf.at[slot], w_sem.at[1, slot]),
            pltpu.make_async_copy(wd_hbm.at[layer, half, cols, :], wd_buf.at[slot], w_sem.at[2, slot]),
        )

    @pl.when(i == 0)
    def _():
        x_copy(0).start()
        for ahead in range(W_AHEAD):
            for c in w_copies(ahead, ahead):
                c.start()
        xs = xs_ref[...]
        h_ref[tm:tm + n_small, :] = _rms(xs, g_ref[...]).astype(BF16)
        os_ref[...] = xs

    x_copy(i).wait()
    x = x_buf[...]
    h_ref[0:tm, :] = _rms(x, g_ref[...]).astype(BF16)
    o_ref[...] = x
    x_copy(jnp.minimum(i + 1, last)).start()

    def walk(rows):
        def two_tiles(jj, carry):
            for u in (0, 1):
                j = 2 * jj + u
                step = i * n_tiles + j
                slot = lax.rem(step, W_SLOTS)
                for c in w_copies(j, slot):
                    c.wait()
                for c in w_copies(lax.rem(j + W_AHEAD, n_tiles), lax.rem(step + W_AHEAD, W_SLOTS)):
                    c.start()
                h = h_ref[0:rows, :]
                gate = _dot(h, wg_buf[slot].astype(BF16))
                up = _dot(h, wu_buf[slot].astype(BF16))
                act = (gate * jax.nn.sigmoid(gate) * up).astype(BF16)
                d = 0.5 * _dot(act, wd_buf[slot].astype(BF16))
                o_ref[...] += d[0:tm]
                if rows > tm:
                    os_ref[...] += d[tm:rows]
            return carry

        lax.fori_loop(0, n_tiles // 2, two_tiles, 0)

    @pl.when(i == 0)
    def _():
        walk(tm + n_small)

    @pl.when(i > 0)
    def _():
        walk(tm)

    @pl.when(i == last)
    def _():
        x_copy(last).wait()
        for ahead in range(W_AHEAD):
            for c in w_copies(ahead, lax.rem((last + 1) * n_tiles + ahead, W_SLOTS)):
                c.wait()


def _ffn(x, xs, g, wg, wu, wd, *, layer, half, tm, tf):
    m, d = x.shape
    n_small = xs.shape[0]
    f = wg.shape[-1]
    assert m % tm == 0 and f % (2 * tf) == 0
    rows = lambda i: (i, 0)
    const = lambda i: (0, 0)
    hbm = pl.BlockSpec(memory_space=pl.ANY)
    return pl.pallas_call(
        functools.partial(_ffn_kernel, layer=layer, half=half, tm=tm, tf=tf),
        grid=(m // tm,),
        in_specs=[hbm, pl.BlockSpec((n_small, d), const), pl.BlockSpec((1, d), const),
                  hbm, hbm, hbm],
        out_specs=[pl.BlockSpec((tm, d), rows), pl.BlockSpec((n_small, d), const)],
        out_shape=[jax.ShapeDtypeStruct((m, d), F32), jax.ShapeDtypeStruct((n_small, d), F32)],
        scratch_shapes=[
            pltpu.VMEM((tm, d), F32),
            pltpu.VMEM((tm + n_small, d), BF16),
            pltpu.VMEM((W_SLOTS, d, tf), F32),
            pltpu.VMEM((W_SLOTS, d, tf), F32),
            pltpu.VMEM((W_SLOTS, tf, d), F32),
            pltpu.SemaphoreType.DMA((1,)),
            pltpu.SemaphoreType.DMA((3, W_SLOTS)),
        ],
        compiler_params=_params(("arbitrary",)),
        name="ffn",
    )(x, xs, g.reshape(1, d), wg, wu, wd)


def _norm_matmul_kernel(x_ref, g_ref, w_ref, o_ref, h_ref):
    @pl.when(pl.program_id(1) == 0)
    def _():
        h_ref[...] = _rms(x_ref[...], g_ref[...]).astype(BF16)

    o_ref[...] = _dot(h_ref[...], w_ref[...])


def _norm_matmul(x, g, w, *, tm, tn):
    m, d = x.shape
    n = w.shape[1]
    return pl.pallas_call(
        _norm_matmul_kernel,
        grid=(m // tm, n // tn),
        in_specs=[
            pl.BlockSpec((tm, d), lambda i, j: (i, 0)),
            pl.BlockSpec((1, d), lambda i, j: (0, 0)),
            pl.BlockSpec((d, tn), lambda i, j: (0, j)),
        ],
        out_specs=pl.BlockSpec((tm, tn), lambda i, j: (i, j)),
        out_shape=jax.ShapeDtypeStruct((m, n), F32),
        scratch_shapes=[pltpu.VMEM((tm, d), BF16)],
        compiler_params=_params(("parallel", "arbitrary")),
        name="norm_matmul",
    )(x, g.reshape(1, d), w)


def _head_norm(p, g):
    cols = []
    for h in range(N_HEADS):
        cols.append(_rms(p[:, h * HEAD_DIM:(h + 1) * HEAD_DIM], g))
    return jnp.concatenate(cols, axis=-1)


def _even_in_kernel(x_ref, g_ref, w_ref, qg_ref, kg_ref, avg_ref, wmix_ref, bmix_ref,
                    q_ref, k_ref, v_ref, gated_ref, *rest, rows):
    *maybe_avn_ref, h_ref = rest
    sec = q_ref.shape[-1]
    h_ref[...] = _rms(x_ref[...], g_ref[...]).astype(BF16)

    def section(s):
        return _dot(h_ref[...], w_ref[:, s * sec:(s + 1) * sec])

    q_ref[...] = _head_norm(section(0), qg_ref[...]).astype(BF16)
    k_ref[...] = _head_norm(section(1), kg_ref[...])
    v_ref[...] = section(2)
    au = jax.nn.gelu(section(3))
    avn = _rms(jax.nn.gelu(section(4)), avg_ref[...])
    if maybe_avn_ref:
        maybe_avn_ref[0][...] = avn
    avb = avn.astype(BF16)
    k_pad = wmix_ref.shape[2] - rows
    for c in range(avn.shape[0] // rows):
        rs = slice(c * rows, (c + 1) * rows)
        for grp in range(N_HEADS):
            cs = slice(grp * HEAD_DIM, (grp + 1) * HEAD_DIM)
            blk = avb[rs, cs]
            if k_pad:
                blk = jnp.concatenate([blk, jnp.zeros((k_pad, HEAD_DIM), BF16)], axis=0)
            mix = _dot(wmix_ref[grp], blk) + bmix_ref[:, cs]
            gated_ref[rs, cs] = (au[rs, cs] * mix).astype(BF16)


def _even_in(x, g, w, qg, kg, avg, wmix, bmix, *, tm, want_avn):
    m, d = x.shape
    rows = wmix.shape[1]
    assert tm % rows == 0
    sec = N_HEADS * HEAD_DIM
    row_blk = lambda i: (i, 0)
    const = lambda i: (0, 0)
    once = dict(pipeline_mode=pl.Buffered(1))
    out_blk = pl.BlockSpec((tm, sec), row_blk)
    return pl.pallas_call(
        functools.partial(_even_in_kernel, rows=rows),
        grid=(m // tm,),
        in_specs=[
            pl.BlockSpec((tm, d), row_blk),
            pl.BlockSpec((1, d), const),
            pl.BlockSpec((d, 5 * sec), const, **once),
            pl.BlockSpec((1, HEAD_DIM), const),
            pl.BlockSpec((1, HEAD_DIM), const),
            pl.BlockSpec((1, sec), const),
            pl.BlockSpec(wmix.shape, lambda i: (0, 0, 0)),
            pl.BlockSpec((rows, sec), const),
        ],
        out_specs=[out_blk] * (5 if want_avn else 4),
        out_shape=[
            jax.ShapeDtypeStruct((m, sec), BF16),
            jax.ShapeDtypeStruct((m, sec), F32),
            jax.ShapeDtypeStruct((m, sec), F32),
            jax.ShapeDtypeStruct((m, sec), BF16),
        ] + [jax.ShapeDtypeStruct((m, sec), F32)] * want_avn,
        scratch_shapes=[pltpu.VMEM((tm, d), BF16)],
        compiler_params=_params(("parallel",)),
        name="even_in",
    )(x, g.reshape(1, d), w, qg.reshape(1, HEAD_DIM), kg.reshape(1, HEAD_DIM),
      avg.reshape(1, sec), wmix, bmix)


def _softplus(z):
    return jnp.maximum(z, 0.0) + jnp.log(1.0 + jnp.exp(-jnp.abs(z)))


def _suffix_sum_weights():
    row = jnp.arange(2 * PAGE)[:, None] % PAGE
    col = jnp.arange(2 * PAGE)[None, :]
    return -jnp.where(col < PAGE, row > col, True).astype(BF16)


def _sb_logits(qs, kbs, biases):
    return [lax.dot_general(q, kb, (((1,), (1,)), ((), ())), preferred_element_type=F32)
            * SB_SCALE + bias for q, kb, bias in zip(qs, kbs, biases)]


def _sb_weights(zs, cum_w, runs, mask, side_work=()):
    rows = zs[0].shape[0]
    sps, parts = [], []
    for c, z in enumerate(zs):
        sp = _softplus(z)
        sps.append(sp)
        sp = sp if mask is None else jnp.where(mask, sp, 0.0)
        hi = sp.astype(BF16)
        lo = (sp - hi.astype(F32)).astype(BF16)
        parts.append(jnp.concatenate([hi, lo], axis=1))
        if side_work:
            side_work[c]()
    cum = _dot(jnp.concatenate(parts, axis=0), cum_w)
    weights, new_runs, run = [], [], None
    for c, (z, sp) in enumerate(zip(zs, sps)):
        run = run if runs[c] is None else runs[c]
        cum_c = cum[c * rows:(c + 1) * rows]
        a = jnp.exp(z - sp + cum_c[:, :PAGE] + run)
        if mask is not None:
            a = jnp.where(mask, a, 0.0)
        weights.append(a.astype(BF16))
        run = run + cum_c[:, PAGE:]
        new_runs.append(run)
    return weights, new_runs


def _sb_prompt_kernel(bias_ref, q_ref, k_ref, v_ref, cw_ref, o_ref,
                      kb_ref, vb_ref, acc_ref, run_ref, *, heads):
    hb = pl.program_id(1)
    i = pl.program_id(2)

    @pl.when(i == 0)
    def _():
        kb_ref[...] = k_ref[...].astype(BF16)
        vb_ref[...] = v_ref[...].astype(BF16)

    cum_w = cw_ref[...]
    cols = [slice(hh * HEAD_DIM, (hh + 1) * HEAD_DIM) for hh in range(heads)]
    qs = [q_ref[:, cs] for cs in cols]
    biases = [bias_ref[hb * heads + hh] for hh in range(heads)]

    def logits(block):
        start = pl.multiple_of(block * PAGE, PAGE)
        return _sb_logits(qs, [kb_ref[pl.ds(start, PAGE), cs] for cs in cols], biases)

    def weights(zs, mask, side_work=()):
        a, runs = _sb_weights(zs, cum_w, [run_ref[:, cs] for cs in cols], mask, side_work)
        for run, cs in zip(runs, cols):
            run_ref[:, cs] = run
        return a

    def accumulate(a, block):
        start = pl.multiple_of(block * PAGE, PAGE)
        for a_c, cs in zip(a, cols):
            acc_ref[:, cs] += _dot(a_c, vb_ref[pl.ds(start, PAGE), cs])

    acc_ref[...] = jnp.zeros_like(acc_ref)
    run_ref[...] = jnp.zeros_like(run_ref)
    below_diag = (lax.broadcasted_iota(jnp.int32, (PAGE, PAGE), 1)
                  < lax.broadcasted_iota(jnp.int32, (PAGE, PAGE), 0))
    a_diag = weights(logits(i), below_diag)

    def body(it, carry):
        zs, a_prev = carry
        block = i - 1 - it
        nxt = pl.multiple_of(jnp.maximum(block - 1, 0) * PAGE, PAGE)
        prev = pl.multiple_of((block + 1) * PAGE, PAGE)
        zs_next = [None] * heads

        def side(c):
            def work():
                cs = cols[c]
                zs_next[c], = _sb_logits([qs[c]], [kb_ref[pl.ds(nxt, PAGE), cs]], [biases[c]])
                acc_ref[:, cs] += _dot(a_prev[c], vb_ref[pl.ds(prev, PAGE), cs])
            return work

        a_cur = weights(zs, None, [side(c) for c in range(heads)])
        return tuple(zs_next), tuple(a_cur)

    _, a_last = lax.fori_loop(0, i, body, (tuple(logits(jnp.maximum(i - 1, 0))), tuple(a_diag)))
    accumulate(a_last, 0)
    o_ref[...] = acc_ref[...].astype(BF16)


def _sb_prompt(q, k, v, bias, *, batch, heads):
    m, width = q.shape
    t = m // batch
    nq = t // PAGE
    hw = heads * HEAD_DIM
    qo_spec = pl.BlockSpec((PAGE, hw), lambda b, h, i: (b * nq + i, h))
    kv_spec = pl.BlockSpec((t, hw), lambda b, h, i: (b, h))
    return pl.pallas_call(
        functools.partial(_sb_prompt_kernel, heads=heads),
        grid=(batch, width // hw, nq),
        in_specs=[pl.BlockSpec(memory_space=pltpu.SMEM), qo_spec, kv_spec, kv_spec,
                  pl.BlockSpec((2 * PAGE, 2 * PAGE), lambda b, h, i: (0, 0))],
        out_specs=qo_spec,
        out_shape=jax.ShapeDtypeStruct((m, width), BF16),
        scratch_shapes=[pltpu.VMEM((t, hw), BF16), pltpu.VMEM((t, hw), BF16),
                        pltpu.VMEM((PAGE, hw), F32), pltpu.VMEM((PAGE, hw), F32)],
        compiler_params=_params(("parallel", "parallel", "arbitrary")),
        name="sb_prompt",
    )(bias, q, k, v, _suffix_sum_weights())


def _sb_sample_kernel(pt_ref, q_ref, bias_ref, cw_ref, knew_ref, vnew_ref, *rest, tq, pages):
    kpage_refs, vpage_refs = rest[:pages], rest[pages:2 * pages]
    o_ref, acc_ref, run_ref = rest[2 * pages:]
    j = pl.program_id(1)
    q = q_ref[0]
    bias = bias_ref[...]
    cum_w = cw_ref[...]

    @pl.when(j == 0)
    def _():
        t_pos = lax.rem(lax.broadcasted_iota(jnp.int32, (q.shape[0], PAGE), 0), tq)
        s_pos = lax.broadcasted_iota(jnp.int32, (q.shape[0], PAGE), 1)
        def new_block(ref):
            rows_in = ref.shape[1]
            fill = jnp.zeros((PAGE - rows_in, ref.shape[2]), BF16)
            return jnp.concatenate([ref[0].astype(BF16), fill], axis=0)

        zs = _sb_logits([q], [new_block(knew_ref)], [bias])
        a, runs = _sb_weights(zs, cum_w, [jnp.zeros((q.shape[0], PAGE), F32)], s_pos < t_pos)
        acc_ref[...] = _dot(a[0], new_block(vnew_ref))
        run_ref[...] = runs[0]

    @pl.when(j > 0)
    def _():
        def page(ref):
            heads = [ref[0, pl.ds(h, PAGE, stride=N_HEADS), :] for h in range(N_HEADS)]
            return jnp.concatenate(heads, axis=1).astype(BF16)

        zs = _sb_logits([q] * pages, [page(r) for r in kpage_refs], [bias] * pages)
        a, runs = _sb_weights(zs, cum_w, [run_ref[...]] + [None] * (pages - 1), None)
        ds = [_dot(a_p, page(r)) for a_p, r in zip(a, vpage_refs)]
        acc_ref[...] += functools.reduce(lambda x, y: x + y, ds)
        run_ref[...] = runs[-1]

    @pl.when(j == pl.num_programs(1) - 1)
    def _():
        for h in range(N_HEADS):
            cs = slice(h * HEAD_DIM, (h + 1) * HEAD_DIM)
            o_ref[0, :, cs] = acc_ref[h * tq:(h + 1) * tq, cs].astype(BF16)


def _sb_sample(q, k_new, v_new, cache_k, cache_v, page_table, bias, *, batch, page_base, pages):
    m, width = q.shape
    tq = m // batch
    n_pages = page_table.shape[1]
    assert n_pages % pages == 0
    rows = N_HEADS * tq
    q4 = q.reshape(batch, tq, N_HEADS, HEAD_DIM).transpose(0, 2, 1, 3)
    eye = jnp.eye(N_HEADS, dtype=BF16)
    q_rows = (q4[:, :, :, None, :] * eye[None, :, None, :, None]).reshape(batch, rows, width)
    bias_rows = jnp.broadcast_to(jnp.repeat(bias, tq)[:, None], (rows, PAGE))
    new_rows = -(-tq // BF16_ROWS) * BF16_ROWS
    pad = ((0, 0), (0, new_rows - tq), (0, 0))
    k_pad = jnp.pad(k_new.reshape(batch, tq, width), pad)
    v_pad = jnp.pad(v_new.reshape(batch, tq, width), pad)

    def page_spec(p):
        def index(b, j, pt):
            return (page_base + pt[b, n_pages - 1 - (jnp.maximum(j, 1) - 1) * pages - p], 0, 0)
        return pl.BlockSpec((1, PAGE * N_HEADS, HEAD_DIM), index)

    per_b = lambda b, j, pt: (b, 0, 0)
    const = lambda b, j, pt: (0, 0)
    new_spec = pl.BlockSpec((1, new_rows, width), per_b)
    page_specs = [page_spec(p) for p in range(pages)]
    out = pl.pallas_call(
        functools.partial(_sb_sample_kernel, tq=tq, pages=pages),
        grid_spec=pltpu.PrefetchScalarGridSpec(
            num_scalar_prefetch=1,
            grid=(batch, n_pages // pages + 1),
            in_specs=[
                pl.BlockSpec((1, rows, width), per_b),
                pl.BlockSpec((rows, PAGE), const),
                pl.BlockSpec((2 * PAGE, 2 * PAGE), const),
                new_spec, new_spec, *page_specs, *page_specs,
            ],
            out_specs=pl.BlockSpec((1, tq, width), per_b),
            scratch_shapes=[pltpu.VMEM((rows, width), F32), pltpu.VMEM((rows, PAGE), F32)],
        ),
        out_shape=jax.ShapeDtypeStruct((batch, tq, width), BF16),
        compiler_params=_params(("parallel", "arbitrary")),
        name="sb_sample",
    )(page_table, q_rows, bias_rows, _suffix_sum_weights(), k_pad, v_pad,
      *([cache_k] * pages), *([cache_v] * pages))
    return out.reshape(m, width)


def _out_proj_kernel(a_ref, b_ref, wa_ref, wb_ref, r_ref, o_ref):
    o_ref[...] = r_ref[...] + _dot(a_ref[...], wa_ref[...]) + _dot(b_ref[...], wb_ref[...])


def _out_proj(a, b, w, resid, *, tm, tn):
    m, ka = a.shape
    n = w.shape[1]
    assert b.shape[1] == ka and w.shape[0] == 2 * ka
    return pl.pallas_call(
        _out_proj_kernel,
        grid=(m // tm, n // tn),
        in_specs=[
            pl.BlockSpec((tm, ka), lambda i, j: (i, 0)),
            pl.BlockSpec((tm, ka), lambda i, j: (i, 0)),
            pl.BlockSpec((ka, tn), lambda i, j: (0, j)),
            pl.BlockSpec((ka, tn), lambda i, j: (1, j)),
            pl.BlockSpec((tm, tn), lambda i, j: (i, j)),
        ],
        out_specs=pl.BlockSpec((tm, tn), lambda i, j: (i, j)),
        out_shape=jax.ShapeDtypeStruct((m, n), F32),
        compiler_params=_params(("parallel", "parallel")),
        name="out_proj",
    )(a, b, w, w, resid)


def _odd_mix_kernel(*refs, tt, pos0, project):
    n_in = 3 if project else 1
    (pool_prev_ref, conv_prev_ref, cwg_ref, cscale_ref, convw_ref,
     c_ref, d_ref, pool_out_ref, conv_out_ref, xbuf_ref, ubuf_ref) = refs[n_in:]
    t_idx = pl.program_id(1)
    width = c_ref.shape[-1]

    @pl.when(t_idx == 0)
    def _():
        xbuf_ref[HIST_PAD - POOL_HIST:HIST_PAD, :] = pool_prev_ref[0]
        ubuf_ref[HIST_PAD - CONV_HIST:HIST_PAD, :] = conv_prev_ref[0]

    if project:
        x_ref, g_ref, w_ref = refs[:n_in]
        h = _rms(x_ref[0], g_ref[...]).astype(BF16)
        section = lambda s: _dot(h, w_ref[:, s * width:(s + 1) * width])
    else:
        p_ref, = refs[:n_in]
        section = lambda s: p_ref[0, :, s * width:(s + 1) * width]
    xc = section(0)
    gb = section(1)
    u = section(2) * section(3)
    xbuf_ref[HIST_PAD:HIST_PAD + tt, :] = xc
    ubuf_ref[HIST_PAD:HIST_PAD + tt, :] = u

    seen = pos0 + t_idx * tt + lax.broadcasted_iota(jnp.int32, (tt, 1), 0) + 1
    for grp, win in enumerate(POOL_WINDOWS):
        cs = slice(grp * POOL_GROUP, (grp + 1) * POOL_GROUP)
        s = xc[:, cs]
        for back in range(1, win):
            s = s + xbuf_ref[HIST_PAD - back:HIST_PAD - back + tt, cs]
        count = jnp.minimum(seen, win).astype(F32)
        pooled = s / count - xc[:, cs]
        mixed = _dot(pooled.astype(BF16), cwg_ref[grp]) * cscale_ref[:, cs]
        c_ref[0, :, cs] = mixed.astype(BF16)

    conv = (ubuf_ref[HIST_PAD - 2:HIST_PAD - 2 + tt, :] * convw_ref[0:1, :]
            + ubuf_ref[HIST_PAD - 1:HIST_PAD - 1 + tt, :] * convw_ref[1:2, :]
            + u * convw_ref[2:3, :])
    d_ref[0] = (gb * conv).astype(BF16)

    pool_tail = xbuf_ref[HIST_PAD + tt - POOL_HIST:HIST_PAD + tt, :]
    conv_tail = ubuf_ref[HIST_PAD + tt - CONV_HIST:HIST_PAD + tt, :]
    pool_out_ref[0] = pool_tail
    conv_out_ref[0] = conv_tail
    xbuf_ref[HIST_PAD - POOL_HIST:HIST_PAD, :] = pool_tail
    ubuf_ref[HIST_PAD - CONV_HIST:HIST_PAD, :] = conv_tail


def _odd_mix(rows_in, pool_prev, conv_prev, cwg, cscale, convw, *, tt, pos0):
    batch, t, _ = rows_in[0].shape
    width = pool_prev.shape[-1]
    seq_blk = lambda b, i: (b, i, 0)
    per_b = lambda b, i: (b, 0, 0)
    const = lambda b, i: (0, 0)
    project = len(rows_in) == 3
    if project:
        x, g, w = rows_in
        d = x.shape[-1]
        rows_in = (x, g.reshape(1, d), w)
        rows_specs = [pl.BlockSpec((1, tt, d), seq_blk), pl.BlockSpec((1, d), const),
                      pl.BlockSpec(w.shape, const, pipeline_mode=pl.Buffered(1))]
    else:
        rows_specs = [pl.BlockSpec((1, tt, 4 * width), seq_blk)]
    return pl.pallas_call(
        functools.partial(_odd_mix_kernel, tt=tt, pos0=pos0, project=project),
        grid=(batch, t // tt),
        in_specs=rows_specs + [
            pl.BlockSpec((1, POOL_HIST, width), per_b),
            pl.BlockSpec((1, CONV_HIST, width), per_b),
            pl.BlockSpec(cwg.shape, lambda b, i: (0, 0, 0)),
            pl.BlockSpec((1, width), lambda b, i: (0, 0)),
            pl.BlockSpec(convw.shape, lambda b, i: (0, 0)),
        ],
        out_specs=[
            pl.BlockSpec((1, tt, width), seq_blk),
            pl.BlockSpec((1, tt, width), seq_blk),
            pl.BlockSpec((1, POOL_HIST, width), per_b),
            pl.BlockSpec((1, CONV_HIST, width), per_b),
        ],
        out_shape=[
            jax.ShapeDtypeStruct((batch, t, width), BF16),
            jax.ShapeDtypeStruct((batch, t, width), BF16),
            jax.ShapeDtypeStruct((batch, POOL_HIST, width), F32),
            jax.ShapeDtypeStruct((batch, CONV_HIST, width), F32),
        ],
        scratch_shapes=[pltpu.VMEM((HIST_PAD + tt, width), F32),
                        pltpu.VMEM((HIST_PAD + tt, width), F32)],
        compiler_params=_params(("parallel", "arbitrary")),
        name="odd_mix",
    )(*rows_in, pool_prev, conv_prev, cwg, cscale.reshape(1, width), convw)


def _pick_tile(m, target):
    return target if m % target == 0 else m


def kernel(x_prompt, x_sample, cache_k, cache_v, state_pool, state_conv, page_table, norm_g, ffn_w_gate, ffn_w_up, ffn_w_down, w_in_even, w_out_even, q_norm_g, k_norm_g, sb_bias, a_v_norm_g, a_ws, a_bs, w_in_odd, w_out_odd, c_wg, c_scale, d_conv_w):
    bp, tp, d = x_prompt.shape
    bs, ts, _ = x_sample.shape
    depth = norm_g.shape[0]
    mp, ms = bp * tp, bs * ts
    sec = N_HEADS * HEAD_DIM

    tm_p = _pick_tile(mp, 512)
    n_out = w_out_even.shape[-1]

    def ffn_pair(yp, ys, layer, half):
        return _ffn(yp, ys, norm_g[layer, 2 * half], ffn_w_gate, ffn_w_up, ffn_w_down,
                    layer=layer, half=half, tm=_pick_tile(mp, 1024),
                    tf=_pick_tile(ffn_w_gate.shape[-1], 256))

    yp = x_prompt.reshape(mp, d)
    ys = x_sample.reshape(ms, d)
    outs = {name: [] for name in ("kp", "vp", "ks", "vs", "av", "pp", "ps", "cp", "cs")}

    for layer in range(depth):
        i = layer // 2
        yp, ys = ffn_pair(yp, ys, layer, 0)
        if layer % 2 == 0:
            w_in_b, w_out_b = w_in_even[i].astype(BF16), w_out_even[i].astype(BF16)
            tril_p = jnp.tril(a_ws[i]).astype(BF16)
            bias_p = jnp.repeat(a_bs[i].T, HEAD_DIM, axis=1)
            tril_s = jnp.tril(a_ws[i][:, :ts, :ts])
            eye = jnp.eye(bs, dtype=F32)
            wmix_s = (eye[None, :, None, :, None] * tril_s[:, None, :, None, :]).reshape(
                N_HEADS, ms, ms).astype(BF16)
            wmix_s = jnp.pad(wmix_s, ((0, 0), (0, 0), (0, max(0, HEAD_DIM - ms))))
            bias_s = jnp.tile(jnp.repeat(a_bs[i][:, :ts].T, HEAD_DIM, axis=1), (bs, 1))

            q, k, v, gated = _even_in(yp, norm_g[layer, 1], w_in_b, q_norm_g[i],
                                      k_norm_g[i], a_v_norm_g[i], tril_p, bias_p,
                                      tm=_pick_tile(mp, 512), want_avn=False)
            sb = _sb_prompt(q, k, v, sb_bias[i], batch=bp, heads=8)
            yp = _out_proj(gated, sb, w_out_b, yp, tm=tm_p, tn=n_out)
            outs["kp"].append(k.reshape(bp, tp, N_HEADS, HEAD_DIM))
            outs["vp"].append(v.reshape(bp, tp, N_HEADS, HEAD_DIM))

            q, k, v, gated, avn = _even_in(ys, norm_g[layer, 1], w_in_b, q_norm_g[i],
                                           k_norm_g[i], a_v_norm_g[i], wmix_s, bias_s, tm=ms,
                                           want_avn=True)
            n_phys = cache_k.shape[1]
            pool_shape = (cache_k.shape[0] * n_phys, PAGE * N_HEADS, HEAD_DIM)
            sb = _sb_sample(q, k, v, cache_k.reshape(pool_shape), cache_v.reshape(pool_shape),
                            page_table, sb_bias[i], batch=bs, page_base=i * n_phys, pages=16)
            ys = _out_proj(gated, sb, w_out_b, ys, tm=ms, tn=n_out)
            outs["ks"].append(k.reshape(bs, ts, N_HEADS, HEAD_DIM))
            outs["vs"].append(v.reshape(bs, ts, N_HEADS, HEAD_DIM))
            outs["av"].append(avn.reshape(bs, ts, sec))
        else:
            width = c_scale.shape[-1]
            w_in_b, w_out_b = w_in_odd[i].astype(BF16), w_out_odd[i].astype(BF16)
            cwg_b = c_wg[i].astype(BF16)
            c_out, d_out, pool_st, conv_st = _odd_mix(
                (yp.reshape(bp, tp, d), norm_g[layer, 1], w_in_b),
                jnp.zeros((bp, POOL_HIST, width), F32), jnp.zeros((bp, CONV_HIST, width), F32),
                cwg_b, c_scale[i], d_conv_w[i], tt=_pick_tile(tp, 512), pos0=0)
            yp = _out_proj(c_out.reshape(mp, width), d_out.reshape(mp, width), w_out_b,
                           yp, tm=tm_p, tn=n_out)
            outs["pp"].append(pool_st)
            outs["cp"].append(conv_st)

            p = _norm_matmul(ys, norm_g[layer, 1], w_in_b, tm=ms, tn=1024)
            past = page_table.shape[1] * PAGE
            c_out, d_out, pool_st, conv_st = _odd_mix(
                (p.reshape(bs, ts, 4 * width),), state_pool[i], state_conv[i],
                cwg_b, c_scale[i], d_conv_w[i], tt=ts, pos0=past)
            ys = _out_proj(c_out.reshape(ms, width), d_out.reshape(ms, width), w_out_b,
                           ys, tm=ms, tn=n_out)
            outs["ps"].append(pool_st)
            outs["cs"].append(conv_st)
        yp, ys = ffn_pair(yp, ys, layer, 1)

    return (yp.reshape(bp, tp, d), ys.reshape(bs, ts, d),
            jnp.stack(outs["kp"]), jnp.stack(outs["vp"]),
            jnp.stack(outs["ks"]), jnp.stack(outs["vs"]), jnp.stack(outs["av"]),
            jnp.stack(outs["pp"]), jnp.stack(outs["ps"]),
            jnp.stack(outs["cp"]), jnp.stack(outs["cs"]))
```

```python
import functools

import jax
import jax.numpy as jnp
from jax import lax
from jax.experimental import pallas as pl
from jax.experimental.pallas import tpu as pltpu

F32 = jnp.float32
BF16 = jnp.bfloat16

EPS = 1e-6
HEAD_DIM = 128
N_HEADS = 8
SB_SCALE = HEAD_DIM ** -0.5
PAGE = 128
POOL_WINDOWS = (2, 4, 8, 16)
POOL_GROUP = 256
POOL_HIST = 15
CONV_HIST = 2
HIST_PAD = 16
BF16_ROWS = 16
VMEM_LIMIT = 56 * 1024 * 1024
FFN_ROWS = 1024
FFN_COLS = 256
ROW_TILE = 512
HEADS_PER_STEP = 8
PAGES_PER_STEP = 16
W_AHEAD = 2
W_SLOTS = 2 * W_AHEAD


def _params(sem):
    return pltpu.CompilerParams(dimension_semantics=sem, vmem_limit_bytes=VMEM_LIMIT)


def _rms(x, g):
    return x * lax.rsqrt(jnp.mean(x * x, axis=-1, keepdims=True) + EPS) * g


def _dot(a, b):
    return jnp.dot(a, b, preferred_element_type=F32)


def _ffn_kernel(x_hbm, xs_ref, g_ref, wg_hbm, wu_hbm, wd_hbm, o_ref, os_ref,
                x_buf, h_ref, wg_buf, wu_buf, wd_buf, x_sem, w_sem, *, layer, half, tm, tf):
    i = pl.program_id(0)
    last = pl.num_programs(0) - 1
    n_tiles = wg_hbm.shape[-1] // tf
    n_small = xs_ref.shape[0]

    def x_copy(tile):
        return pltpu.make_async_copy(x_hbm.at[pl.ds(tile * tm, tm), :], x_buf, x_sem.at[0])

    def w_copies(j, slot):
        cols = pl.ds(j * tf, tf)
        return (
            pltpu.make_async_copy(wg_hbm.at[layer, half, :, cols], wg_buf.at[slot], w_sem.at[0, slot]),
            pltpu.make_async_copy(wu_hbm.at[layer, half, :, cols], wu_buf.at[slot], w_sem.at[1, slot]),
            pltpu.make_async_copy(wd_hbm.at[layer, half, cols, :], wd_buf.at[slot], w_sem.at[2, slot]),
        )

    @pl.when(i == 0)
    def _():
        x_copy(0).start()
        for ahead in range(W_AHEAD):
            for c in w_copies(ahead, ahead):
                c.start()
        xs = xs_ref[...]
        h_ref[tm:tm + n_small, :] = _rms(xs, g_ref[...]).astype(BF16)
        os_ref[...] = xs

    x_copy(i).wait()
    x = x_buf[...]
    h_ref[0:tm, :] = _rms(x, g_ref[...]).astype(BF16)
    o_ref[...] = x
    x_copy(jnp.minimum(i + 1, last)).start()

    def walk(rows):
        def trip(jj, carry):
            first = W_AHEAD * jj
            step = i * n_tiles + first
            slots = [lax.rem(step + u, W_SLOTS) for u in range(W_AHEAD)]
            for u in range(W_AHEAD):
                for c in w_copies(first + u, slots[u]):
                    c.wait()
            for u in range(W_AHEAD):
                for c in w_copies(lax.rem(first + W_AHEAD + u, n_tiles),
                                  lax.rem(step + W_AHEAD + u, W_SLOTS)):
                    c.start()
            for slot in slots:
                h = h_ref[0:rows, :]
                gate = _dot(h, wg_buf[slot].astype(BF16))
                up = _dot(h, wu_buf[slot].astype(BF16))
                act = (gate * jax.nn.sigmoid(gate) * up).astype(BF16)
                d = 0.5 * _dot(act, wd_buf[slot].astype(BF16))
                o_ref[...] += d[0:tm]
                if rows > tm:
                    os_ref[...] += d[tm:rows]
            return carry

        lax.fori_loop(0, n_tiles // W_AHEAD, trip, 0)

    @pl.when(i == 0)
    def _():
        walk(tm + n_small)

    @pl.when(i > 0)
    def _():
        walk(tm)

    @pl.when(i == last)
    def _():
        x_copy(last).wait()
        for ahead in range(W_AHEAD):
            for c in w_copies(ahead, lax.rem((last + 1) * n_tiles + ahead, W_SLOTS)):
                c.wait()


def _ffn(x, xs, g, wg, wu, wd, *, layer, half, tm, tf):
    m, d = x.shape
    n_small = xs.shape[0]
    f = wg.shape[-1]
    assert m % tm == 0 and f % (W_AHEAD * tf) == 0
    rows = lambda i: (i, 0)
    const = lambda i: (0, 0)
    hbm = pl.BlockSpec(memory_space=pl.ANY)
    return pl.pallas_call(
        functools.partial(_ffn_kernel, layer=layer, half=half, tm=tm, tf=tf),
        grid=(m // tm,),
        in_specs=[hbm, pl.BlockSpec((n_small, d), const), pl.BlockSpec((1, d), const),
                  hbm, hbm, hbm],
        out_specs=[pl.BlockSpec((tm, d), rows), pl.BlockSpec((n_small, d), const)],
        out_shape=[jax.ShapeDtypeStruct((m, d), F32), jax.ShapeDtypeStruct((n_small, d), F32)],
        scratch_shapes=[
            pltpu.VMEM((tm, d), F32),
            pltpu.VMEM((tm + n_small, d), BF16),
            pltpu.VMEM((W_SLOTS, d, tf), F32),
            pltpu.VMEM((W_SLOTS, d, tf), F32),
            pltpu.VMEM((W_SLOTS, tf, d), F32),
            pltpu.SemaphoreType.DMA((1,)),
            pltpu.SemaphoreType.DMA((3, W_SLOTS)),
        ],
        compiler_params=_params(("arbitrary",)),
        name="ffn",
    )(x, xs, g.reshape(1, d), wg, wu, wd)


def _norm_matmul_kernel(x_ref, g_ref, w_ref, o_ref, h_ref):
    @pl.when(pl.program_id(1) == 0)
    def _():
        h_ref[...] = _rms(x_ref[...], g_ref[...]).astype(BF16)

    o_ref[...] = _dot(h_ref[...], w_ref[...])


def _norm_matmul(x, g, w, *, tm, tn):
    m, d = x.shape
    n = w.shape[1]
    return pl.pallas_call(
        _norm_matmul_kernel,
        grid=(m // tm, n // tn),
        in_specs=[
            pl.BlockSpec((tm, d), lambda i, j: (i, 0)),
            pl.BlockSpec((1, d), lambda i, j: (0, 0)),
            pl.BlockSpec((d, tn), lambda i, j: (0, j)),
        ],
        out_specs=pl.BlockSpec((tm, tn), lambda i, j: (i, j)),
        out_shape=jax.ShapeDtypeStruct((m, n), F32),
        scratch_shapes=[pltpu.VMEM((tm, d), BF16)],
        compiler_params=_params(("parallel", "arbitrary")),
        name="norm_matmul",
    )(x, g.reshape(1, d), w)


def _head_norm(p, g):
    cols = []
    for h in range(N_HEADS):
        cols.append(_rms(p[:, h * HEAD_DIM:(h + 1) * HEAD_DIM], g))
    return jnp.concatenate(cols, axis=-1)


def _even_in_kernel(x_ref, g_ref, w_ref, qg_ref, kg_ref, avg_ref, wmix_ref, bmix_ref,
                    q_ref, k_ref, v_ref, gated_ref, *rest, rows):
    *maybe_avn_ref, h_ref = rest
    sec = q_ref.shape[-1]
    h_ref[...] = _rms(x_ref[...], g_ref[...]).astype(BF16)

    def section(s):
        return _dot(h_ref[...], w_ref[:, s * sec:(s + 1) * sec])

    q_ref[...] = _head_norm(section(0), qg_ref[...]).astype(BF16)
    k_ref[...] = _head_norm(section(1), kg_ref[...])
    v_ref[...] = section(2)
    au = jax.nn.gelu(section(3))
    avn = _rms(jax.nn.gelu(section(4)), avg_ref[...])
    if maybe_avn_ref:
        maybe_avn_ref[0][...] = avn
    avb = avn.astype(BF16)
    k_pad = wmix_ref.shape[2] - rows
    for c in range(avn.shape[0] // rows):
        rs = slice(c * rows, (c + 1) * rows)
        for grp in range(N_HEADS):
            cs = slice(grp * HEAD_DIM, (grp + 1) * HEAD_DIM)
            blk = avb[rs, cs]
            if k_pad:
                blk = jnp.concatenate([blk, jnp.zeros((k_pad, HEAD_DIM), BF16)], axis=0)
            mix = _dot(wmix_ref[grp], blk) + bmix_ref[:, cs]
            gated_ref[rs, cs] = (au[rs, cs] * mix).astype(BF16)


def _even_in(x, g, w, qg, kg, avg, wmix, bmix, *, tm, want_avn):
    m, d = x.shape
    rows = wmix.shape[1]
    assert tm % rows == 0
    sec = N_HEADS * HEAD_DIM
    row_blk = lambda i: (i, 0)
    const = lambda i: (0, 0)
    once = dict(pipeline_mode=pl.Buffered(1))
    out_blk = pl.BlockSpec((tm, sec), row_blk)
    return pl.pallas_call(
        functools.partial(_even_in_kernel, rows=rows),
        grid=(m // tm,),
        in_specs=[
            pl.BlockSpec((tm, d), row_blk),
            pl.BlockSpec((1, d), const),
            pl.BlockSpec((d, 5 * sec), const, **once),
            pl.BlockSpec((1, HEAD_DIM), const),
            pl.BlockSpec((1, HEAD_DIM), const),
            pl.BlockSpec((1, sec), const),
            pl.BlockSpec(wmix.shape, lambda i: (0, 0, 0)),
            pl.BlockSpec((rows, sec), const),
        ],
        out_specs=[out_blk] * (5 if want_avn else 4),
        out_shape=[
            jax.ShapeDtypeStruct((m, sec), BF16),
            jax.ShapeDtypeStruct((m, sec), F32),
            jax.ShapeDtypeStruct((m, sec), F32),
            jax.ShapeDtypeStruct((m, sec), BF16),
        ] + [jax.ShapeDtypeStruct((m, sec), F32)] * want_avn,
        scratch_shapes=[pltpu.VMEM((tm, d), BF16)],
        compiler_params=_params(("parallel",)),
        name="even_in",
    )(x, g.reshape(1, d), w, qg.reshape(1, HEAD_DIM), kg.reshape(1, HEAD_DIM),
      avg.reshape(1, sec), wmix, bmix)


def _softplus(z):
    return jnp.maximum(z, 0.0) + jnp.log(1.0 + jnp.exp(-jnp.abs(z)))


def _suffix_sum_weights():
    row = jnp.arange(2 * PAGE)[:, None] % PAGE
    col = jnp.arange(2 * PAGE)[None, :]
    return -jnp.where(col < PAGE, row > col, True).astype(BF16)


def _sb_logits(qs, kbs, biases):
    return [lax.dot_general(q, kb, (((1,), (1,)), ((), ())), preferred_element_type=F32)
            * SB_SCALE + bias for q, kb, bias in zip(qs, kbs, biases)]


def _sb_weights(zs, cum_w, runs, mask, side_work=()):
    rows = zs[0].shape[0]
    sps, parts = [], []
    for c, z in enumerate(zs):
        sp = _softplus(z)
        sps.append(sp)
        sp = sp if mask is None else jnp.where(mask, sp, 0.0)
        hi = sp.astype(BF16)
        lo = (sp - hi.astype(F32)).astype(BF16)
        parts.append(jnp.concatenate([hi, lo], axis=1))
        if side_work:
            side_work[c]()
    cum = _dot(jnp.concatenate(parts, axis=0), cum_w)
    weights, new_runs, run = [], [], None
    for c, (z, sp) in enumerate(zip(zs, sps)):
        run = run if runs[c] is None else runs[c]
        cum_c = cum[c * rows:(c + 1) * rows]
        a = jnp.exp(z - sp + cum_c[:, :PAGE] + run)
        if mask is not None:
            a = jnp.where(mask, a, 0.0)
        weights.append(a.astype(BF16))
        run = run + cum_c[:, PAGE:]
        new_runs.append(run)
    return weights, new_runs


def _sb_prompt_kernel(bias_ref, q_ref, k_ref, v_ref, cw_ref, o_ref,
                      kb_ref, vb_ref, acc_ref, run_ref, *, heads):
    hb = pl.program_id(1)
    i = pl.program_id(2)

    @pl.when(i == 0)
    def _():
        kb_ref[...] = k_ref[...].astype(BF16)
        vb_ref[...] = v_ref[...].astype(BF16)

    cum_w = cw_ref[...]
    cols = [slice(hh * HEAD_DIM, (hh + 1) * HEAD_DIM) for hh in range(heads)]
    qs = [q_ref[:, cs] for cs in cols]
    biases = [bias_ref[hb * heads + hh] for hh in range(heads)]

    def logits(block):
        start = pl.multiple_of(block * PAGE, PAGE)
        return _sb_logits(qs, [kb_ref[pl.ds(start, PAGE), cs] for cs in cols], biases)

    def weights(zs, mask, side_work=()):
        a, runs = _sb_weights(zs, cum_w, [run_ref[:, cs] for cs in cols], mask, side_work)
        for run, cs in zip(runs, cols):
            run_ref[:, cs] = run
        return a

    def accumulate(a, block):
        start = pl.multiple_of(block * PAGE, PAGE)
        for a_c, cs in zip(a, cols):
            acc_ref[:, cs] += _dot(a_c, vb_ref[pl.ds(start, PAGE), cs])

    acc_ref[...] = jnp.zeros_like(acc_ref)
    run_ref[...] = jnp.zeros_like(run_ref)
    below_diag = (lax.broadcasted_iota(jnp.int32, (PAGE, PAGE), 1)
                  < lax.broadcasted_iota(jnp.int32, (PAGE, PAGE), 0))
    a_diag = weights(logits(i), below_diag)

    def body(it, carry):
        zs, a_prev = carry
        block = i - 1 - it
        nxt = pl.multiple_of(jnp.maximum(block - 1, 0) * PAGE, PAGE)
        prev = pl.multiple_of((block + 1) * PAGE, PAGE)
        zs_next = [None] * heads

        def side(c):
            def work():
                cs = cols[c]
                zs_next[c], = _sb_logits([qs[c]], [kb_ref[pl.ds(nxt, PAGE), cs]], [biases[c]])
                acc_ref[:, cs] += _dot(a_prev[c], vb_ref[pl.ds(prev, PAGE), cs])
            return work

        a_cur = weights(zs, None, [side(c) for c in range(heads)])
        return tuple(zs_next), tuple(a_cur)

    _, a_last = lax.fori_loop(0, i, body, (tuple(logits(jnp.maximum(i - 1, 0))), tuple(a_diag)))
    accumulate(a_last, 0)
    o_ref[...] = acc_ref[...].astype(BF16)


def _sb_prompt(q, k, v, bias, *, batch, heads):
    m, width = q.shape
    t = m // batch
    nq = t // PAGE
    hw = heads * HEAD_DIM
    qo_spec = pl.BlockSpec((PAGE, hw), lambda b, h, i: (b * nq + i, h))
    kv_spec = pl.BlockSpec((t, hw), lambda b, h, i: (b, h))
    return pl.pallas_call(
        functools.partial(_sb_prompt_kernel, heads=heads),
        grid=(batch, width // hw, nq),
        in_specs=[pl.BlockSpec(memory_space=pltpu.SMEM), qo_spec, kv_spec, kv_spec,
                  pl.BlockSpec((2 * PAGE, 2 * PAGE), lambda b, h, i: (0, 0))],
        out_specs=qo_spec,
        out_shape=jax.ShapeDtypeStruct((m, width), BF16),
        scratch_shapes=[pltpu.VMEM((t, hw), BF16), pltpu.VMEM((t, hw), BF16),
                        pltpu.VMEM((PAGE, hw), F32), pltpu.VMEM((PAGE, hw), F32)],
        compiler_params=_params(("parallel", "parallel", "arbitrary")),
        name="sb_prompt",
    )(bias, q, k, v, _suffix_sum_weights())


def _sb_sample_kernel(pt_ref, q_ref, bias_ref, cw_ref, knew_ref, vnew_ref, *rest, tq, pages):
    kpage_refs, vpage_refs = rest[:pages], rest[pages:2 * pages]
    o_ref, acc_ref, run_ref = rest[2 * pages:]
    j = pl.program_id(1)
    q = q_ref[0]
    bias = bias_ref[...]
    cum_w = cw_ref[...]

    @pl.when(j == 0)
    def _():
        t_pos = lax.rem(lax.broadcasted_iota(jnp.int32, (q.shape[0], PAGE), 0), tq)
        s_pos = lax.broadcasted_iota(jnp.int32, (q.shape[0], PAGE), 1)
        def new_block(ref):
            rows_in = ref.shape[1]
            fill = jnp.zeros((PAGE - rows_in, ref.shape[2]), BF16)
            return jnp.concatenate([ref[0].astype(BF16), fill], axis=0)

        zs = _sb_logits([q], [new_block(knew_ref)], [bias])
        a, runs = _sb_weights(zs, cum_w, [jnp.zeros((q.shape[0], PAGE), F32)], s_pos < t_pos)
        acc_ref[...] = _dot(a[0], new_block(vnew_ref))
        run_ref[...] = runs[0]

    @pl.when(j > 0)
    def _():
        def page(ref):
            heads = [ref[0, pl.ds(h, PAGE, stride=N_HEADS), :] for h in range(N_HEADS)]
            return jnp.concatenate(heads, axis=1).astype(BF16)

        zs = _sb_logits([q] * pages, [page(r) for r in kpage_refs], [bias] * pages)
        a, runs = _sb_weights(zs, cum_w, [run_ref[...]] + [None] * (pages - 1), None)
        ds = [_dot(a_p, page(r)) for a_p, r in zip(a, vpage_refs)]
        acc_ref[...] += functools.reduce(lambda x, y: x + y, ds)
        run_ref[...] = runs[-1]

    @pl.when(j == pl.num_programs(1) - 1)
    def _():
        for h in range(N_HEADS):
            cs = slice(h * HEAD_DIM, (h + 1) * HEAD_DIM)
            o_ref[0, :, cs] = acc_ref[h * tq:(h + 1) * tq, cs].astype(BF16)


def _sb_sample(q, k_new, v_new, cache_k, cache_v, page_table, bias, *, batch, page_base, pages):
    m, width = q.shape
    tq = m // batch
    n_pages = page_table.shape[1]
    assert n_pages % pages == 0
    rows = N_HEADS * tq
    q4 = q.reshape(batch, tq, N_HEADS, HEAD_DIM).transpose(0, 2, 1, 3)
    eye = jnp.eye(N_HEADS, dtype=BF16)
    q_rows = (q4[:, :, :, None, :] * eye[None, :, None, :, None]).reshape(batch, rows, width)
    bias_rows = jnp.broadcast_to(jnp.repeat(bias, tq)[:, None], (rows, PAGE))
    new_rows = -(-tq // BF16_ROWS) * BF16_ROWS
    pad = ((0, 0), (0, new_rows - tq), (0, 0))
    k_pad = jnp.pad(k_new.reshape(batch, tq, width), pad)
    v_pad = jnp.pad(v_new.reshape(batch, tq, width), pad)

    def page_spec(p):
        def index(b, j, pt):
            return (page_base + pt[b, n_pages - 1 - (jnp.maximum(j, 1) - 1) * pages - p], 0, 0)
        return pl.BlockSpec((1, PAGE * N_HEADS, HEAD_DIM), index)

    per_b = lambda b, j, pt: (b, 0, 0)
    const = lambda b, j, pt: (0, 0)
    new_spec = pl.BlockSpec((1, new_rows, width), per_b)
    page_specs = [page_spec(p) for p in range(pages)]
    out = pl.pallas_call(
        functools.partial(_sb_sample_kernel, tq=tq, pages=pages),
        grid_spec=pltpu.PrefetchScalarGridSpec(
            num_scalar_prefetch=1,
            grid=(batch, n_pages // pages + 1),
            in_specs=[
                pl.BlockSpec((1, rows, width), per_b),
                pl.BlockSpec((rows, PAGE), const),
                pl.BlockSpec((2 * PAGE, 2 * PAGE), const),
                new_spec, new_spec, *page_specs, *page_specs,
            ],
            out_specs=pl.BlockSpec((1, tq, width), per_b),
            scratch_shapes=[pltpu.VMEM((rows, width), F32), pltpu.VMEM((rows, PAGE), F32)],
        ),
        out_shape=jax.ShapeDtypeStruct((batch, tq, width), BF16),
        compiler_params=_params(("parallel", "arbitrary")),
        name="sb_sample",
    )(page_table, q_rows, bias_rows, _suffix_sum_weights(), k_pad, v_pad,
      *([cache_k] * pages), *([cache_v] * pages))
    return out.reshape(m, width)


def _out_proj_kernel(a_ref, b_ref, wa_ref, wb_ref, r_ref, o_ref):
    o_ref[...] = r_ref[...] + _dot(a_ref[...], wa_ref[...]) + _dot(b_ref[...], wb_ref[...])


def _out_proj(a, b, w, resid, *, tm, tn):
    m, ka = a.shape
    n = w.shape[1]
    assert b.shape[1] == ka and w.shape[0] == 2 * ka
    return pl.pallas_call(
        _out_proj_kernel,
        grid=(m // tm, n // tn),
        in_specs=[
            pl.BlockSpec((tm, ka), lambda i, j: (i, 0)),
            pl.BlockSpec((tm, ka), lambda i, j: (i, 0)),
            pl.BlockSpec((ka, tn), lambda i, j: (0, j)),
            pl.BlockSpec((ka, tn), lambda i, j: (1, j)),
            pl.BlockSpec((tm, tn), lambda i, j: (i, j)),
        ],
        out_specs=pl.BlockSpec((tm, tn), lambda i, j: (i, j)),
        out_shape=jax.ShapeDtypeStruct((m, n), F32),
        compiler_params=_params(("parallel", "parallel")),
        name="out_proj",
    )(a, b, w, w, resid)


def _odd_mix_kernel(*refs, tt, pos0, project):
    n_in = 3 if project else 1
    (pool_prev_ref, conv_prev_ref, cwg_ref, cscale_ref, convw_ref,
     c_ref, d_ref, pool_out_ref, conv_out_ref, xbuf_ref, ubuf_ref) = refs[n_in:]
    t_idx = pl.program_id(1)
    width = c_ref.shape[-1]

    @pl.when(t_idx == 0)
    def _():
        xbuf_ref[HIST_PAD - POOL_HIST:HIST_PAD, :] = pool_prev_ref[0]
        ubuf_ref[HIST_PAD - CONV_HIST:HIST_PAD, :] = conv_prev_ref[0]

    if project:
        x_ref, g_ref, w_ref = refs[:n_in]
        h = _rms(x_ref[0], g_ref[...]).astype(BF16)
        section = lambda s: _dot(h, w_ref[:, s * width:(s + 1) * width])
    else:
        p_ref, = refs[:n_in]
        section = lambda s: p_ref[0, :, s * width:(s + 1) * width]
    xc = section(0)
    gb = section(1)
    u = section(2) * section(3)
    xbuf_ref[HIST_PAD:HIST_PAD + tt, :] = xc
    ubuf_ref[HIST_PAD:HIST_PAD + tt, :] = u

    seen = pos0 + t_idx * tt + lax.broadcasted_iota(jnp.int32, (tt, 1), 0) + 1
    for grp, win in enumerate(POOL_WINDOWS):
        cs = slice(grp * POOL_GROUP, (grp + 1) * POOL_GROUP)
        s = xc[:, cs]
        for back in range(1, win):
            s = s + xbuf_ref[HIST_PAD - back:HIST_PAD - back + tt, cs]
        count = jnp.minimum(seen, win).astype(F32)
        pooled = s / count - xc[:, cs]
        mixed = _dot(pooled.astype(BF16), cwg_ref[grp]) * cscale_ref[:, cs]
        c_ref[0, :, cs] = mixed.astype(BF16)

    conv = (ubuf_ref[HIST_PAD - 2:HIST_PAD - 2 + tt, :] * convw_ref[0:1, :]
            + ubuf_ref[HIST_PAD - 1:HIST_PAD - 1 + tt, :] * convw_ref[1:2, :]
            + u * convw_ref[2:3, :])
    d_ref[0] = (gb * conv).astype(BF16)

    pool_tail = xbuf_ref[HIST_PAD + tt - POOL_HIST:HIST_PAD + tt, :]
    conv_tail = ubuf_ref[HIST_PAD + tt - CONV_HIST:HIST_PAD + tt, :]
    pool_out_ref[0] = pool_tail
    conv_out_ref[0] = conv_tail
    xbuf_ref[HIST_PAD - POOL_HIST:HIST_PAD, :] = pool_tail
    ubuf_ref[HIST_PAD - CONV_HIST:HIST_PAD, :] = conv_tail


def _odd_mix(rows_in, pool_prev, conv_prev, cwg, cscale, convw, *, tt, pos0):
    batch, t, _ = rows_in[0].shape
    width = pool_prev.shape[-1]
    seq_blk = lambda b, i: (b, i, 0)
    per_b = lambda b, i: (b, 0, 0)
    const = lambda b, i: (0, 0)
    project = len(rows_in) == 3
    if project:
        x, g, w = rows_in
        d = x.shape[-1]
        rows_in = (x, g.reshape(1, d), w)
        rows_specs = [pl.BlockSpec((1, tt, d), seq_blk), pl.BlockSpec((1, d), const),
                      pl.BlockSpec(w.shape, const, pipeline_mode=pl.Buffered(1))]
    else:
        rows_specs = [pl.BlockSpec((1, tt, 4 * width), seq_blk)]
    return pl.pallas_call(
        functools.partial(_odd_mix_kernel, tt=tt, pos0=pos0, project=project),
        grid=(batch, t // tt),
        in_specs=rows_specs + [
            pl.BlockSpec((1, POOL_HIST, width), per_b),
            pl.BlockSpec((1, CONV_HIST, width), per_b),
            pl.BlockSpec(cwg.shape, lambda b, i: (0, 0, 0)),
            pl.BlockSpec((1, width), lambda b, i: (0, 0)),
            pl.BlockSpec(convw.shape, lambda b, i: (0, 0)),
        ],
        out_specs=[
            pl.BlockSpec((1, tt, width), seq_blk),
            pl.BlockSpec((1, tt, width), seq_blk),
            pl.BlockSpec((1, POOL_HIST, width), per_b),
            pl.BlockSpec((1, CONV_HIST, width), per_b),
        ],
        out_shape=[
            jax.ShapeDtypeStruct((batch, t, width), BF16),
            jax.ShapeDtypeStruct((batch, t, width), BF16),
            jax.ShapeDtypeStruct((batch, POOL_HIST, width), F32),
            jax.ShapeDtypeStruct((batch, CONV_HIST, width), F32),
        ],
        scratch_shapes=[pltpu.VMEM((HIST_PAD + tt, width), F32),
                        pltpu.VMEM((HIST_PAD + tt, width), F32)],
        compiler_params=_params(("parallel", "arbitrary")),
        name="odd_mix",
    )(*rows_in, pool_prev, conv_prev, cwg, cscale.reshape(1, width), convw)


def _pick_tile(m, target):
    return target if m % target == 0 else m


def kernel(x_prompt, x_sample, cache_k, cache_v, state_pool, state_conv, page_table, norm_g, ffn_w_gate, ffn_w_up, ffn_w_down, w_in_even, w_out_even, q_norm_g, k_norm_g, sb_bias, a_v_norm_g, a_ws, a_bs, w_in_odd, w_out_odd, c_wg, c_scale, d_conv_w):
    bp, tp, d = x_prompt.shape
    bs, ts, _ = x_sample.shape
    depth = norm_g.shape[0]
    mp, ms = bp * tp, bs * ts
    sec = N_HEADS * HEAD_DIM

    tm_p = _pick_tile(mp, ROW_TILE)
    n_out = w_out_even.shape[-1]

    def ffn_pair(yp, ys, layer, half):
        return _ffn(yp, ys, norm_g[layer, 2 * half], ffn_w_gate, ffn_w_up, ffn_w_down,
                    layer=layer, half=half, tm=_pick_tile(mp, FFN_ROWS),
                    tf=_pick_tile(ffn_w_gate.shape[-1], FFN_COLS))

    yp = x_prompt.reshape(mp, d)
    ys = x_sample.reshape(ms, d)
    outs = {name: [] for name in ("kp", "vp", "ks", "vs", "av", "pp", "ps", "cp", "cs")}

    for layer in range(depth):
        i = layer // 2
        yp, ys = ffn_pair(yp, ys, layer, 0)
        if layer % 2 == 0:
            w_in_b, w_out_b = w_in_even[i].astype(BF16), w_out_even[i].astype(BF16)
            tril_p = jnp.tril(a_ws[i]).astype(BF16)
            bias_p = jnp.repeat(a_bs[i].T, HEAD_DIM, axis=1)
            tril_s = jnp.tril(a_ws[i][:, :ts, :ts])
            eye = jnp.eye(bs, dtype=F32)
            wmix_s = (eye[None, :, None, :, None] * tril_s[:, None, :, None, :]).reshape(
                N_HEADS, ms, ms).astype(BF16)
            wmix_s = jnp.pad(wmix_s, ((0, 0), (0, 0), (0, max(0, HEAD_DIM - ms))))
            bias_s = jnp.tile(jnp.repeat(a_bs[i][:, :ts].T, HEAD_DIM, axis=1), (bs, 1))

            q, k, v, gated = _even_in(yp, norm_g[layer, 1], w_in_b, q_norm_g[i],
                                      k_norm_g[i], a_v_norm_g[i], tril_p, bias_p,
                                      tm=tm_p, want_avn=False)
            sb = _sb_prompt(q, k, v, sb_bias[i], batch=bp, heads=HEADS_PER_STEP)
            yp = _out_proj(gated, sb, w_out_b, yp, tm=tm_p, tn=n_out)
            outs["kp"].append(k.reshape(bp, tp, N_HEADS, HEAD_DIM))
            outs["vp"].append(v.reshape(bp, tp, N_HEADS, HEAD_DIM))

            q, k, v, gated, avn = _even_in(ys, norm_g[layer, 1], w_in_b, q_norm_g[i],
                                           k_norm_g[i], a_v_norm_g[i], wmix_s, bias_s, tm=ms,
                                           want_avn=True)
            n_phys = cache_k.shape[1]
            pool_shape = (cache_k.shape[0] * n_phys, PAGE * N_HEADS, HEAD_DIM)
            sb = _sb_sample(q, k, v, cache_k.reshape(pool_shape), cache_v.reshape(pool_shape),
                            page_table, sb_bias[i], batch=bs, page_base=i * n_phys,
                            pages=PAGES_PER_STEP)
            ys = _out_proj(gated, sb, w_out_b, ys, tm=ms, tn=n_out)
            outs["ks"].append(k.reshape(bs, ts, N_HEADS, HEAD_DIM))
            outs["vs"].append(v.reshape(bs, ts, N_HEADS, HEAD_DIM))
            outs["av"].append(avn.reshape(bs, ts, sec))
        else:
            width = c_scale.shape[-1]
            w_in_b, w_out_b = w_in_odd[i].astype(BF16), w_out_odd[i].astype(BF16)
            cwg_b = c_wg[i].astype(BF16)
            c_out, d_out, pool_st, conv_st = _odd_mix(
                (yp.reshape(bp, tp, d), norm_g[layer, 1], w_in_b),
                jnp.zeros((bp, POOL_HIST, width), F32), jnp.zeros((bp, CONV_HIST, width), F32),
                cwg_b, c_scale[i], d_conv_w[i], tt=_pick_tile(tp, ROW_TILE), pos0=0)
            yp = _out_proj(c_out.reshape(mp, width), d_out.reshape(mp, width), w_out_b,
                           yp, tm=tm_p, tn=n_out)
            outs["pp"].append(pool_st)
            outs["cp"].append(conv_st)

            p = _norm_matmul(ys, norm_g[layer, 1], w_in_b, tm=ms, tn=width)
            past = page_table.shape[1] * PAGE
            c_out, d_out, pool_st, conv_st = _odd_mix(
                (p.reshape(bs, ts, 4 * width),), state_pool[i], state_conv[i],
                cwg_b, c_scale[i], d_conv_w[i], tt=ts, pos0=past)
            ys = _out_proj(c_out.reshape(ms, width), d_out.reshape(ms, width), w_out_b,
                           ys, tm=ms, tn=n_out)
            outs["ps"].append(pool_st)
            outs["cs"].append(conv_st)
        yp, ys = ffn_pair(yp, ys, layer, 1)

    return (yp.reshape(bp, tp, d), ys.reshape(bs, ts, d),
            jnp.stack(outs["kp"]), jnp.stack(outs["vp"]),
            jnp.stack(outs["ks"]), jnp.stack(outs["vs"]), jnp.stack(outs["av"]),
            jnp.stack(outs["pp"]), jnp.stack(outs["ps"]),
            jnp.stack(outs["cp"]), jnp.stack(outs["cs"]))
```

```python
import functools

import jax
import jax.numpy as jnp
from jax import lax
from jax.experimental import pallas as pl
from jax.experimental.pallas import tpu as pltpu

F32 = jnp.float32
BF16 = jnp.bfloat16

EPS = 1e-6
HEAD_DIM = 128
N_HEADS = 8
SB_SCALE = HEAD_DIM ** -0.5
PAGE = 128
POOL_WINDOWS = (2, 4, 8, 16)
POOL_GROUP = 256
POOL_HIST = 15
CONV_HIST = 2
HIST_PAD = 16
BF16_ROWS = 16
VMEM_LIMIT = 56 * 1024 * 1024
FFN_ROWS = 1024
FFN_COLS = 256
ROW_TILE = 512
HEADS_PER_STEP = 8
PAGES_PER_STEP = 16
FFN_TRIP = 2
FFN_SLOTS = 2


def _params(sem):
    return pltpu.CompilerParams(dimension_semantics=sem, vmem_limit_bytes=VMEM_LIMIT)


def _rms(x, g):
    return x * lax.rsqrt(jnp.mean(x * x, axis=-1, keepdims=True) + EPS) * g


def _dot(a, b):
    return jnp.dot(a, b, preferred_element_type=F32)


def _ffn_kernel(x_hbm, xs_ref, g_ref, wg_hbm, wu_hbm, wd_hbm, o_ref, os_ref,
                x_buf, h_ref, wg_buf, wu_buf, wd_buf, x_sem, w_sem, *, layer, half, tm, tf):
    i = pl.program_id(0)
    last = pl.num_programs(0) - 1
    n_tiles = wg_hbm.shape[-1] // tf
    n_small = xs_ref.shape[0]

    def x_copy(tile):
        return pltpu.make_async_copy(x_hbm.at[pl.ds(tile * tm, tm), :], x_buf, x_sem.at[0])

    n_trips = n_tiles // FFN_TRIP

    def w_copies(trip_idx, slot):
        cols = pl.ds(trip_idx * FFN_TRIP * tf, FFN_TRIP * tf)
        return (
            pltpu.make_async_copy(wg_hbm.at[layer, half, :, cols], wg_buf.at[slot], w_sem.at[0, slot]),
            pltpu.make_async_copy(wu_hbm.at[layer, half, :, cols], wu_buf.at[slot], w_sem.at[1, slot]),
            pltpu.make_async_copy(wd_hbm.at[layer, half, cols, :], wd_buf.at[slot], w_sem.at[2, slot]),
        )

    @pl.when(i == 0)
    def _():
        x_copy(0).start()
        for c in w_copies(0, 0):
            c.start()
        xs = xs_ref[...]
        h_ref[tm:tm + n_small, :] = _rms(xs, g_ref[...]).astype(BF16)
        os_ref[...] = xs

    x_copy(i).wait()
    x = x_buf[...]
    h_ref[0:tm, :] = _rms(x, g_ref[...]).astype(BF16)
    o_ref[...] = x
    x_copy(jnp.minimum(i + 1, last)).start()

    def walk(rows):
        def trip(jj, carry):
            step = i * n_trips + jj
            slot = lax.rem(step, FFN_SLOTS)
            for c in w_copies(jj, slot):
                c.wait()
            for c in w_copies(lax.rem(jj + 1, n_trips), lax.rem(step + 1, FFN_SLOTS)):
                c.start()
            for u in range(FFN_TRIP):
                cols = slice(u * tf, (u + 1) * tf)
                h = h_ref[0:rows, :]
                gate = _dot(h, wg_buf[slot, :, cols].astype(BF16))
                up = _dot(h, wu_buf[slot, :, cols].astype(BF16))
                act = (gate * jax.nn.sigmoid(gate) * up).astype(BF16)
                d = 0.5 * _dot(act, wd_buf[slot, cols, :].astype(BF16))
                o_ref[...] += d[0:tm]
                if rows > tm:
                    os_ref[...] += d[tm:rows]
            return carry

        lax.fori_loop(0, n_trips, trip, 0)

    @pl.when(i == 0)
    def _():
        walk(tm + n_small)

    @pl.when(i > 0)
    def _():
        walk(tm)

    @pl.when(i == last)
    def _():
        x_copy(last).wait()
        for c in w_copies(0, lax.rem((last + 1) * n_trips, FFN_SLOTS)):
            c.wait()


def _ffn(x, xs, g, wg, wu, wd, *, layer, half, tm, tf):
    m, d = x.shape
    n_small = xs.shape[0]
    f = wg.shape[-1]
    assert m % tm == 0 and f % (FFN_TRIP * tf) == 0
    rows = lambda i: (i, 0)
    const = lambda i: (0, 0)
    hbm = pl.BlockSpec(memory_space=pl.ANY)
    return pl.pallas_call(
        functools.partial(_ffn_kernel, layer=layer, half=half, tm=tm, tf=tf),
        grid=(m // tm,),
        in_specs=[hbm, pl.BlockSpec((n_small, d), const), pl.BlockSpec((1, d), const),
                  hbm, hbm, hbm],
        out_specs=[pl.BlockSpec((tm, d), rows), pl.BlockSpec((n_small, d), const)],
        out_shape=[jax.ShapeDtypeStruct((m, d), F32), jax.ShapeDtypeStruct((n_small, d), F32)],
        scratch_shapes=[
            pltpu.VMEM((tm, d), F32),
            pltpu.VMEM((tm + n_small, d), BF16),
            pltpu.VMEM((FFN_SLOTS, d, FFN_TRIP * tf), F32),
            pltpu.VMEM((FFN_SLOTS, d, FFN_TRIP * tf), F32),
            pltpu.VMEM((FFN_SLOTS, FFN_TRIP * tf, d), F32),
            pltpu.SemaphoreType.DMA((1,)),
            pltpu.SemaphoreType.DMA((3, FFN_SLOTS)),
        ],
        compiler_params=_params(("arbitrary",)),
        name="ffn",
    )(x, xs, g.reshape(1, d), wg, wu, wd)


def _norm_matmul_kernel(x_ref, g_ref, w_ref, o_ref, h_ref):
    @pl.when(pl.program_id(1) == 0)
    def _():
        h_ref[...] = _rms(x_ref[...], g_ref[...]).astype(BF16)

    o_ref[...] = _dot(h_ref[...], w_ref[...])


def _norm_matmul(x, g, w, *, tm, tn):
    m, d = x.shape
    n = w.shape[1]
    return pl.pallas_call(
        _norm_matmul_kernel,
        grid=(m // tm, n // tn),
        in_specs=[
            pl.BlockSpec((tm, d), lambda i, j: (i, 0)),
            pl.BlockSpec((1, d), lambda i, j: (0, 0)),
            pl.BlockSpec((d, tn), lambda i, j: (0, j)),
        ],
        out_specs=pl.BlockSpec((tm, tn), lambda i, j: (i, j)),
        out_shape=jax.ShapeDtypeStruct((m, n), F32),
        scratch_shapes=[pltpu.VMEM((tm, d), BF16)],
        compiler_params=_params(("parallel", "arbitrary")),
        name="norm_matmul",
    )(x, g.reshape(1, d), w)


def _head_norm(p, g):
    cols = []
    for h in range(N_HEADS):
        cols.append(_rms(p[:, h * HEAD_DIM:(h + 1) * HEAD_DIM], g))
    return jnp.concatenate(cols, axis=-1)


def _even_in_kernel(x_ref, g_ref, w_ref, qg_ref, kg_ref, avg_ref, wmix_ref, bmix_ref,
                    q_ref, k_ref, v_ref, gated_ref, *rest, rows):
    *maybe_avn_ref, h_ref = rest
    sec = q_ref.shape[-1]
    h_ref[...] = _rms(x_ref[...], g_ref[...]).astype(BF16)

    def section(s):
        return _dot(h_ref[...], w_ref[:, s * sec:(s + 1) * sec])

    q_ref[...] = _head_norm(section(0), qg_ref[...]).astype(BF16)
    k_ref[...] = _head_norm(section(1), kg_ref[...])
    v_ref[...] = section(2)
    au = jax.nn.gelu(section(3))
    avn = _rms(jax.nn.gelu(section(4)), avg_ref[...])
    if maybe_avn_ref:
        maybe_avn_ref[0][...] = avn
    avb = avn.astype(BF16)
    k_pad = wmix_ref.shape[2] - rows
    for c in range(avn.shape[0] // rows):
        rs = slice(c * rows, (c + 1) * rows)
        for grp in range(N_HEADS):
            cs = slice(grp * HEAD_DIM, (grp + 1) * HEAD_DIM)
            blk = avb[rs, cs]
            if k_pad:
                blk = jnp.concatenate([blk, jnp.zeros((k_pad, HEAD_DIM), BF16)], axis=0)
            mix = _dot(wmix_ref[grp], blk) + bmix_ref[:, cs]
            gated_ref[rs, cs] = (au[rs, cs] * mix).astype(BF16)


def _even_in(x, g, w, qg, kg, avg, wmix, bmix, *, tm, want_avn):
    m, d = x.shape
    rows = wmix.shape[1]
    assert tm % rows == 0
    sec = N_HEADS * HEAD_DIM
    row_blk = lambda i: (i, 0)
    const = lambda i: (0, 0)
    once = dict(pipeline_mode=pl.Buffered(1))
    out_blk = pl.BlockSpec((tm, sec), row_blk)
    return pl.pallas_call(
        functools.partial(_even_in_kernel, rows=rows),
        grid=(m // tm,),
        in_specs=[
            pl.BlockSpec((tm, d), row_blk),
            pl.BlockSpec((1, d), const),
            pl.BlockSpec((d, 5 * sec), const, **once),
            pl.BlockSpec((1, HEAD_DIM), const),
            pl.BlockSpec((1, HEAD_DIM), const),
            pl.BlockSpec((1, sec), const),
            pl.BlockSpec(wmix.shape, lambda i: (0, 0, 0)),
            pl.BlockSpec((rows, sec), const),
        ],
        out_specs=[out_blk] * (5 if want_avn else 4),
        out_shape=[
            jax.ShapeDtypeStruct((m, sec), BF16),
            jax.ShapeDtypeStruct((m, sec), F32),
            jax.ShapeDtypeStruct((m, sec), F32),
            jax.ShapeDtypeStruct((m, sec), BF16),
        ] + [jax.ShapeDtypeStruct((m, sec), F32)] * want_avn,
        scratch_shapes=[pltpu.VMEM((tm, d), BF16)],
        compiler_params=_params(("parallel",)),
        name="even_in",
    )(x, g.reshape(1, d), w, qg.reshape(1, HEAD_DIM), kg.reshape(1, HEAD_DIM),
      avg.reshape(1, sec), wmix, bmix)


def _softplus(z):
    return jnp.maximum(z, 0.0) + jnp.log(1.0 + jnp.exp(-jnp.abs(z)))


def _suffix_sum_weights():
    row = jnp.arange(2 * PAGE)[:, None] % PAGE
    col = jnp.arange(2 * PAGE)[None, :]
    return -jnp.where(col < PAGE, row > col, True).astype(BF16)


def _sb_logits(qs, kbs, biases):
    return [lax.dot_general(q, kb, (((1,), (1,)), ((), ())), preferred_element_type=F32)
            * SB_SCALE + bias for q, kb, bias in zip(qs, kbs, biases)]


def _sb_weights(zs, cum_w, runs, mask, side_work=()):
    rows = zs[0].shape[0]
    sps, parts = [], []
    for c, z in enumerate(zs):
        sp = _softplus(z)
        sps.append(sp)
        sp = sp if mask is None else jnp.where(mask, sp, 0.0)
        hi = sp.astype(BF16)
        lo = (sp - hi.astype(F32)).astype(BF16)
        parts.append(jnp.concatenate([hi, lo], axis=1))
        if side_work:
            side_work[c]()
    cum = _dot(jnp.concatenate(parts, axis=0), cum_w)
    weights, new_runs, run = [], [], None
    for c, (z, sp) in enumerate(zip(zs, sps)):
        run = run if runs[c] is None else runs[c]
        cum_c = cum[c * rows:(c + 1) * rows]
        a = jnp.exp(z - sp + cum_c[:, :PAGE] + run)
        if mask is not None:
            a = jnp.where(mask, a, 0.0)
        weights.append(a.astype(BF16))
        run = run + cum_c[:, PAGE:]
        new_runs.append(run)
    return weights, new_runs


def _sb_prompt_kernel(bias_ref, q_ref, k_ref, v_ref, cw_ref, o_ref,
                      kb_ref, vb_ref, acc_ref, run_ref, *, heads):
    hb = pl.program_id(1)
    i = pl.program_id(2)

    @pl.when(i == 0)
    def _():
        kb_ref[...] = k_ref[...].astype(BF16)
        vb_ref[...] = v_ref[...].astype(BF16)

    cum_w = cw_ref[...]
    cols = [slice(hh * HEAD_DIM, (hh + 1) * HEAD_DIM) for hh in range(heads)]
    qs = [q_ref[:, cs] for cs in cols]
    biases = [bias_ref[hb * heads + hh] for hh in range(heads)]

    def logits(block):
        start = pl.multiple_of(block * PAGE, PAGE)
        return _sb_logits(qs, [kb_ref[pl.ds(start, PAGE), cs] for cs in cols], biases)

    def weights(zs, mask, side_work=()):
        a, runs = _sb_weights(zs, cum_w, [run_ref[:, cs] for cs in cols], mask, side_work)
        for run, cs in zip(runs, cols):
            run_ref[:, cs] = run
        return a

    def accumulate(a, block):
        start = pl.multiple_of(block * PAGE, PAGE)
        for a_c, cs in zip(a, cols):
            acc_ref[:, cs] += _dot(a_c, vb_ref[pl.ds(start, PAGE), cs])

    acc_ref[...] = jnp.zeros_like(acc_ref)
    run_ref[...] = jnp.zeros_like(run_ref)
    below_diag = (lax.broadcasted_iota(jnp.int32, (PAGE, PAGE), 1)
                  < lax.broadcasted_iota(jnp.int32, (PAGE, PAGE), 0))
    a_diag = weights(logits(i), below_diag)

    def body(it, carry):
        zs, a_prev = carry
        block = i - 1 - it
        nxt = pl.multiple_of(jnp.maximum(block - 1, 0) * PAGE, PAGE)
        prev = pl.multiple_of((block + 1) * PAGE, PAGE)
        zs_next = [None] * heads

        def side(c):
            def work():
                cs = cols[c]
                zs_next[c], = _sb_logits([qs[c]], [kb_ref[pl.ds(nxt, PAGE), cs]], [biases[c]])
                acc_ref[:, cs] += _dot(a_prev[c], vb_ref[pl.ds(prev, PAGE), cs])
            return work

        a_cur = weights(zs, None, [side(c) for c in range(heads)])
        return tuple(zs_next), tuple(a_cur)

    _, a_last = lax.fori_loop(0, i, body, (tuple(logits(jnp.maximum(i - 1, 0))), tuple(a_diag)))
    accumulate(a_last, 0)
    o_ref[...] = acc_ref[...].astype(BF16)


def _sb_prompt(q, k, v, bias, *, batch, heads):
    m, width = q.shape
    t = m // batch
    nq = t // PAGE
    hw = heads * HEAD_DIM
    qo_spec = pl.BlockSpec((PAGE, hw), lambda b, h, i: (b * nq + i, h))
    kv_spec = pl.BlockSpec((t, hw), lambda b, h, i: (b, h))
    return pl.pallas_call(
        functools.partial(_sb_prompt_kernel, heads=heads),
        grid=(batch, width // hw, nq),
        in_specs=[pl.BlockSpec(memory_space=pltpu.SMEM), qo_spec, kv_spec, kv_spec,
                  pl.BlockSpec((2 * PAGE, 2 * PAGE), lambda b, h, i: (0, 0))],
        out_specs=qo_spec,
        out_shape=jax.ShapeDtypeStruct((m, width), BF16),
        scratch_shapes=[pltpu.VMEM((t, hw), BF16), pltpu.VMEM((t, hw), BF16),
                        pltpu.VMEM((PAGE, hw), F32), pltpu.VMEM((PAGE, hw), F32)],
        compiler_params=_params(("parallel", "parallel", "arbitrary")),
        name="sb_prompt",
    )(bias, q, k, v, _suffix_sum_weights())


def _sb_sample_kernel(pt_ref, q_ref, bias_ref, cw_ref, knew_ref, vnew_ref, *rest, tq, pages):
    kpage_refs, vpage_refs = rest[:pages], rest[pages:2 * pages]
    o_ref, acc_ref, run_ref = rest[2 * pages:]
    j = pl.program_id(1)
    q = q_ref[0]
    bias = bias_ref[...]
    cum_w = cw_ref[...]

    @pl.when(j == 0)
    def _():
        t_pos = lax.rem(lax.broadcasted_iota(jnp.int32, (q.shape[0], PAGE), 0), tq)
        s_pos = lax.broadcasted_iota(jnp.int32, (q.shape[0], PAGE), 1)
        def new_block(ref):
            rows_in = ref.shape[1]
            fill = jnp.zeros((PAGE - rows_in, ref.shape[2]), BF16)
            return jnp.concatenate([ref[0].astype(BF16), fill], axis=0)

        zs = _sb_logits([q], [new_block(knew_ref)], [bias])
        a, runs = _sb_weights(zs, cum_w, [jnp.zeros((q.shape[0], PAGE), F32)], s_pos < t_pos)
        acc_ref[...] = _dot(a[0], new_block(vnew_ref))
        run_ref[...] = runs[0]

    @pl.when(j > 0)
    def _():
        def page(ref):
            heads = [ref[0, pl.ds(h, PAGE, stride=N_HEADS), :] for h in range(N_HEADS)]
            return jnp.concatenate(heads, axis=1).astype(BF16)

        zs = _sb_logits([q] * pages, [page(r) for r in kpage_refs], [bias] * pages)
        a, runs = _sb_weights(zs, cum_w, [run_ref[...]] + [None] * (pages - 1), None)
        ds = [_dot(a_p, page(r)) for a_p, r in zip(a, vpage_refs)]
        acc_ref[...] += functools.reduce(lambda x, y: x + y, ds)
        run_ref[...] = runs[-1]

    @pl.when(j == pl.num_programs(1) - 1)
    def _():
        for h in range(N_HEADS):
            cs = slice(h * HEAD_DIM, (h + 1) * HEAD_DIM)
            o_ref[0, :, cs] = acc_ref[h * tq:(h + 1) * tq, cs].astype(BF16)


def _sb_sample(q, k_new, v_new, cache_k, cache_v, page_table, bias, *, batch, page_base, pages):
    m, width = q.shape
    tq = m // batch
    n_pages = page_table.shape[1]
    assert n_pages % pages == 0
    rows = N_HEADS * tq
    q4 = q.reshape(batch, tq, N_HEADS, HEAD_DIM).transpose(0, 2, 1, 3)
    eye = jnp.eye(N_HEADS, dtype=BF16)
    q_rows = (q4[:, :, :, None, :] * eye[None, :, None, :, None]).reshape(batch, rows, width)
    bias_rows = jnp.broadcast_to(jnp.repeat(bias, tq)[:, None], (rows, PAGE))
    new_rows = -(-tq // BF16_ROWS) * BF16_ROWS
    pad = ((0, 0), (0, new_rows - tq), (0, 0))
    k_pad = jnp.pad(k_new.reshape(batch, tq, width), pad)
    v_pad = jnp.pad(v_new.reshape(batch, tq, width), pad)

    def page_spec(p):
        def index(b, j, pt):
            return (page_base + pt[b, n_pages - 1 - (jnp.maximum(j, 1) - 1) * pages - p], 0, 0)
        return pl.BlockSpec((1, PAGE * N_HEADS, HEAD_DIM), index)

    per_b = lambda b, j, pt: (b, 0, 0)
    const = lambda b, j, pt: (0, 0)
    new_spec = pl.BlockSpec((1, new_rows, width), per_b)
    page_specs = [page_spec(p) for p in range(pages)]
    out = pl.pallas_call(
        functools.partial(_sb_sample_kernel, tq=tq, pages=pages),
        grid_spec=pltpu.PrefetchScalarGridSpec(
            num_scalar_prefetch=1,
            grid=(batch, n_pages // pages + 1),
            in_specs=[
                pl.BlockSpec((1, rows, width), per_b),
                pl.BlockSpec((rows, PAGE), const),
                pl.BlockSpec((2 * PAGE, 2 * PAGE), const),
                new_spec, new_spec, *page_specs, *page_specs,
            ],
            out_specs=pl.BlockSpec((1, tq, width), per_b),
            scratch_shapes=[pltpu.VMEM((rows, width), F32), pltpu.VMEM((rows, PAGE), F32)],
        ),
        out_shape=jax.ShapeDtypeStruct((batch, tq, width), BF16),
        compiler_params=_params(("parallel", "arbitrary")),
        name="sb_sample",
    )(page_table, q_rows, bias_rows, _suffix_sum_weights(), k_pad, v_pad,
      *([cache_k] * pages), *([cache_v] * pages))
    return out.reshape(m, width)


def _out_proj_kernel(a_ref, b_ref, wa_ref, wb_ref, r_ref, o_ref):
    o_ref[...] = r_ref[...] + _dot(a_ref[...], wa_ref[...]) + _dot(b_ref[...], wb_ref[...])


def _out_proj(a, b, w, resid, *, tm, tn):
    m, ka = a.shape
    n = w.shape[1]
    assert b.shape[1] == ka and w.shape[0] == 2 * ka
    once = dict(pipeline_mode=pl.Buffered(1)) if tn == n else {}
    return pl.pallas_call(
        _out_proj_kernel,
        grid=(m // tm, n // tn),
        in_specs=[
            pl.BlockSpec((tm, ka), lambda i, j: (i, 0)),
            pl.BlockSpec((tm, ka), lambda i, j: (i, 0)),
            pl.BlockSpec((ka, tn), lambda i, j: (0, j), **once),
            pl.BlockSpec((ka, tn), lambda i, j: (1, j), **once),
            pl.BlockSpec((tm, tn), lambda i, j: (i, j)),
        ],
        out_specs=pl.BlockSpec((tm, tn), lambda i, j: (i, j)),
        out_shape=jax.ShapeDtypeStruct((m, n), F32),
        compiler_params=_params(("parallel", "parallel")),
        name="out_proj",
    )(a, b, w, w, resid)


def _odd_mix_kernel(*refs, tt, pos0, project):
    n_in = 3 if project else 1
    (pool_prev_ref, conv_prev_ref, cwg_ref, cscale_ref, convw_ref,
     c_ref, d_ref, pool_out_ref, conv_out_ref, xbuf_ref, ubuf_ref) = refs[n_in:]
    t_idx = pl.program_id(1)
    width = c_ref.shape[-1]

    @pl.when(t_idx == 0)
    def _():
        xbuf_ref[HIST_PAD - POOL_HIST:HIST_PAD, :] = pool_prev_ref[0]
        ubuf_ref[HIST_PAD - CONV_HIST:HIST_PAD, :] = conv_prev_ref[0]

    if project:
        x_ref, g_ref, w_ref = refs[:n_in]
        h = _rms(x_ref[0], g_ref[...]).astype(BF16)
        section = lambda s: _dot(h, w_ref[:, s * width:(s + 1) * width])
    else:
        p_ref, = refs[:n_in]
        section = lambda s: p_ref[0, :, s * width:(s + 1) * width]
    xc = section(0)
    gb = section(1)
    u = section(2) * section(3)
    xbuf_ref[HIST_PAD:HIST_PAD + tt, :] = xc
    ubuf_ref[HIST_PAD:HIST_PAD + tt, :] = u

    seen = pos0 + t_idx * tt + lax.broadcasted_iota(jnp.int32, (tt, 1), 0) + 1
    for grp, win in enumerate(POOL_WINDOWS):
        cs = slice(grp * POOL_GROUP, (grp + 1) * POOL_GROUP)
        s = xc[:, cs]
        for back in range(1, win):
            s = s + xbuf_ref[HIST_PAD - back:HIST_PAD - back + tt, cs]
        count = jnp.minimum(seen, win).astype(F32)
        pooled = s / count - xc[:, cs]
        mixed = _dot(pooled.astype(BF16), cwg_ref[grp]) * cscale_ref[:, cs]
        c_ref[0, :, cs] = mixed.astype(BF16)

    conv = (ubuf_ref[HIST_PAD - 2:HIST_PAD - 2 + tt, :] * convw_ref[0:1, :]
            + ubuf_ref[HIST_PAD - 1:HIST_PAD - 1 + tt, :] * convw_ref[1:2, :]
            + u * convw_ref[2:3, :])
    d_ref[0] = (gb * conv).astype(BF16)

    pool_tail = xbuf_ref[HIST_PAD + tt - POOL_HIST:HIST_PAD + tt, :]
    conv_tail = ubuf_ref[HIST_PAD + tt - CONV_HIST:HIST_PAD + tt, :]
    pool_out_ref[0] = pool_tail
    conv_out_ref[0] = conv_tail
    xbuf_ref[HIST_PAD - POOL_HIST:HIST_PAD, :] = pool_tail
    ubuf_ref[HIST_PAD - CONV_HIST:HIST_PAD, :] = conv_tail


def _odd_mix(rows_in, pool_prev, conv_prev, cwg, cscale, convw, *, tt, pos0):
    batch, t, _ = rows_in[0].shape
    width = pool_prev.shape[-1]
    seq_blk = lambda b, i: (b, i, 0)
    per_b = lambda b, i: (b, 0, 0)
    const = lambda b, i: (0, 0)
    project = len(rows_in) == 3
    if project:
        x, g, w = rows_in
        d = x.shape[-1]
        rows_in = (x, g.reshape(1, d), w)
        rows_specs = [pl.BlockSpec((1, tt, d), seq_blk), pl.BlockSpec((1, d), const),
                      pl.BlockSpec(w.shape, const, pipeline_mode=pl.Buffered(1))]
    else:
        rows_specs = [pl.BlockSpec((1, tt, 4 * width), seq_blk)]
    return pl.pallas_call(
        functools.partial(_odd_mix_kernel, tt=tt, pos0=pos0, project=project),
        grid=(batch, t // tt),
        in_specs=rows_specs + [
            pl.BlockSpec((1, POOL_HIST, width), per_b),
            pl.BlockSpec((1, CONV_HIST, width), per_b),
            pl.BlockSpec(cwg.shape, lambda b, i: (0, 0, 0)),
            pl.BlockSpec((1, width), lambda b, i: (0, 0)),
            pl.BlockSpec(convw.shape, lambda b, i: (0, 0)),
        ],
        out_specs=[
            pl.BlockSpec((1, tt, width), seq_blk),
            pl.BlockSpec((1, tt, width), seq_blk),
            pl.BlockSpec((1, POOL_HIST, width), per_b),
            pl.BlockSpec((1, CONV_HIST, width), per_b),
        ],
        out_shape=[
            jax.ShapeDtypeStruct((batch, t, width), BF16),
            jax.ShapeDtypeStruct((batch, t, width), BF16),
            jax.ShapeDtypeStruct((batch, POOL_HIST, width), F32),
            jax.ShapeDtypeStruct((batch, CONV_HIST, width), F32),
        ],
        scratch_shapes=[pltpu.VMEM((HIST_PAD + tt, width), F32),
                        pltpu.VMEM((HIST_PAD + tt, width), F32)],
        compiler_params=_params(("parallel", "arbitrary")),
        name="odd_mix",
    )(*rows_in, pool_prev, conv_prev, cwg, cscale.reshape(1, width), convw)


def _pick_tile(m, target):
    return target if m % target == 0 else m


def kernel(x_prompt, x_sample, cache_k, cache_v, state_pool, state_conv, page_table, norm_g, ffn_w_gate, ffn_w_up, ffn_w_down, w_in_even, w_out_even, q_norm_g, k_norm_g, sb_bias, a_v_norm_g, a_ws, a_bs, w_in_odd, w_out_odd, c_wg, c_scale, d_conv_w):
    bp, tp, d = x_prompt.shape
    bs, ts, _ = x_sample.shape
    depth = norm_g.shape[0]
    mp, ms = bp * tp, bs * ts
    sec = N_HEADS * HEAD_DIM

    tm_p = _pick_tile(mp, ROW_TILE)
    n_out = w_out_even.shape[-1]

    def ffn_pair(yp, ys, layer, half):
        return _ffn(yp, ys, norm_g[layer, 2 * half], ffn_w_gate, ffn_w_up, ffn_w_down,
                    layer=layer, half=half, tm=_pick_tile(mp, FFN_ROWS),
                    tf=_pick_tile(ffn_w_gate.shape[-1], FFN_COLS))

    yp = x_prompt.reshape(mp, d)
    ys = x_sample.reshape(ms, d)
    outs = {name: [] for name in ("kp", "vp", "ks", "vs", "av", "pp", "ps", "cp", "cs")}

    for layer in range(depth):
        i = layer // 2
        yp, ys = ffn_pair(yp, ys, layer, 0)
        if layer % 2 == 0:
            w_in_b, w_out_b = w_in_even[i].astype(BF16), w_out_even[i].astype(BF16)
            tril_p = jnp.tril(a_ws[i]).astype(BF16)
            bias_p = jnp.repeat(a_bs[i].T, HEAD_DIM, axis=1)
            tril_s = jnp.tril(a_ws[i][:, :ts, :ts])
            eye = jnp.eye(bs, dtype=F32)
            wmix_s = (eye[None, :, None, :, None] * tril_s[:, None, :, None, :]).reshape(
                N_HEADS, ms, ms).astype(BF16)
            wmix_s = jnp.pad(wmix_s, ((0, 0), (0, 0), (0, max(0, HEAD_DIM - ms))))
            bias_s = jnp.tile(jnp.repeat(a_bs[i][:, :ts].T, HEAD_DIM, axis=1), (bs, 1))

            q, k, v, gated = _even_in(yp, norm_g[layer, 1], w_in_b, q_norm_g[i],
                                      k_norm_g[i], a_v_norm_g[i], tril_p, bias_p,
                                      tm=tm_p, want_avn=False)
            sb = _sb_prompt(q, k, v, sb_bias[i], batch=bp, heads=HEADS_PER_STEP)
            yp = _out_proj(gated, sb, w_out_b, yp, tm=tm_p, tn=n_out)
            outs["kp"].append(k.reshape(bp, tp, N_HEADS, HEAD_DIM))
            outs["vp"].append(v.reshape(bp, tp, N_HEADS, HEAD_DIM))

            q, k, v, gated, avn = _even_in(ys, norm_g[layer, 1], w_in_b, q_norm_g[i],
                                           k_norm_g[i], a_v_norm_g[i], wmix_s, bias_s, tm=ms,
                                           want_avn=True)
            n_phys = cache_k.shape[1]
            pool_shape = (cache_k.shape[0] * n_phys, PAGE * N_HEADS, HEAD_DIM)
            sb = _sb_sample(q, k, v, cache_k.reshape(pool_shape), cache_v.reshape(pool_shape),
                            page_table, sb_bias[i], batch=bs, page_base=i * n_phys,
                            pages=PAGES_PER_STEP)
            ys = _out_proj(gated, sb, w_out_b, ys, tm=ms, tn=n_out)
            outs["ks"].append(k.reshape(bs, ts, N_HEADS, HEAD_DIM))
            outs["vs"].append(v.reshape(bs, ts, N_HEADS, HEAD_DIM))
            outs["av"].append(avn.reshape(bs, ts, sec))
        else:
            width = c_scale.shape[-1]
            w_in_b, w_out_b = w_in_odd[i].astype(BF16), w_out_odd[i].astype(BF16)
            cwg_b = c_wg[i].astype(BF16)
            c_out, d_out, pool_st, conv_st = _odd_mix(
                (yp.reshape(bp, tp, d), norm_g[layer, 1], w_in_b),
                jnp.zeros((bp, POOL_HIST, width), F32), jnp.zeros((bp, CONV_HIST, width), F32),
                cwg_b, c_scale[i], d_conv_w[i], tt=_pick_tile(tp, ROW_TILE), pos0=0)
            yp = _out_proj(c_out.reshape(mp, width), d_out.reshape(mp, width), w_out_b,
                           yp, tm=tm_p, tn=n_out)
            outs["pp"].append(pool_st)
            outs["cp"].append(conv_st)

            p = _norm_matmul(ys, norm_g[layer, 1], w_in_b, tm=ms, tn=width)
            past = page_table.shape[1] * PAGE
            c_out, d_out, pool_st, conv_st = _odd_mix(
                (p.reshape(bs, ts, 4 * width),), state_pool[i], state_conv[i],
                cwg_b, c_scale[i], d_conv_w[i], tt=ts, pos0=past)
            ys = _out_proj(c_out.reshape(ms, width), d_out.reshape(ms, width), w_out_b,
                           ys, tm=ms, tn=n_out)
            outs["ps"].append(pool_st)
            outs["cs"].append(conv_st)
        yp, ys = ffn_pair(yp, ys, layer, 1)

    return (yp.reshape(bp, tp, d), ys.reshape(bs, ts, d),
            jnp.stack(outs["kp"]), jnp.stack(outs["vp"]),
            jnp.stack(outs["ks"]), jnp.stack(outs["vs"]), jnp.stack(outs["av"]),
            jnp.stack(outs["pp"]), jnp.stack(outs["ps"]),
            jnp.stack(outs["cp"]), jnp.stack(outs["cs"]))
```

```python
import functools

import jax
import jax.numpy as jnp
from jax import lax
from jax.experimental import pallas as pl
from jax.experimental.pallas import tpu as pltpu

F32 = jnp.float32
BF16 = jnp.bfloat16

EPS = 1e-6
HEAD_DIM = 128
N_HEADS = 8
SB_SCALE = HEAD_DIM ** -0.5
PAGE = 128
POOL_WINDOWS = (2, 4, 8, 16)
POOL_GROUP = 256
POOL_HIST = 15
CONV_HIST = 2
HIST_PAD = 16
BF16_ROWS = 16
VMEM_LIMIT = 56 * 1024 * 1024
FFN_ROWS = 1024
FFN_COLS = 256
ROW_TILE = 512
HEADS_PER_STEP = 8
PAGES_PER_STEP = 16
FFN_TRIP = 2
FFN_SLOTS = 2


def _params(sem):
    return pltpu.CompilerParams(dimension_semantics=sem, vmem_limit_bytes=VMEM_LIMIT)


def _rms(x, g):
    return x * lax.rsqrt(jnp.mean(x * x, axis=-1, keepdims=True) + EPS) * g


def _dot(a, b):
    return jnp.dot(a, b, preferred_element_type=F32)


def _ffn_kernel(x_hbm, xs_ref, g_ref, wg_hbm, wu_hbm, wd_hbm, o_ref, os_ref,
                x_buf, h_ref, wg_buf, wu_buf, wd_buf, x_sem, w_sem, *, layer, half, tm, tf):
    i = pl.program_id(0)
    last = pl.num_programs(0) - 1
    n_tiles = wg_hbm.shape[-1] // tf
    n_small = xs_ref.shape[0]

    def x_copy(tile):
        return pltpu.make_async_copy(x_hbm.at[pl.ds(tile * tm, tm), :], x_buf, x_sem.at[0])

    n_trips = n_tiles // FFN_TRIP

    def w_copies(trip_idx, slot):
        cols = pl.ds(trip_idx * FFN_TRIP * tf, FFN_TRIP * tf)
        return (
            pltpu.make_async_copy(wg_hbm.at[layer, half, :, cols], wg_buf.at[slot], w_sem.at[0, slot]),
            pltpu.make_async_copy(wu_hbm.at[layer, half, :, cols], wu_buf.at[slot], w_sem.at[1, slot]),
            pltpu.make_async_copy(wd_hbm.at[layer, half, cols, :], wd_buf.at[slot], w_sem.at[2, slot]),
        )

    @pl.when(i == 0)
    def _():
        x_copy(0).start()
        for c in w_copies(0, 0):
            c.start()
        xs = xs_ref[...]
        h_ref[tm:tm + n_small, :] = _rms(xs, g_ref[...]).astype(BF16)
        os_ref[...] = xs

    x_copy(i).wait()
    x = x_buf[...]
    h_ref[0:tm, :] = _rms(x, g_ref[...]).astype(BF16)
    o_ref[...] = x
    x_copy(jnp.minimum(i + 1, last)).start()

    def walk(rows):
        def trip(jj, carry):
            step = i * n_trips + jj
            slot = lax.rem(step, FFN_SLOTS)
            for c in w_copies(jj, slot):
                c.wait()
            for c in w_copies(lax.rem(jj + 1, n_trips), lax.rem(step + 1, FFN_SLOTS)):
                c.start()
            for u in range(FFN_TRIP):
                cols = slice(u * tf, (u + 1) * tf)
                h = h_ref[0:rows, :]
                gate = _dot(h, wg_buf[slot, :, cols].astype(BF16))
                up = _dot(h, wu_buf[slot, :, cols].astype(BF16))
                act = (gate * jax.nn.sigmoid(gate) * up).astype(BF16)
                d = 0.5 * _dot(act, wd_buf[slot, cols, :].astype(BF16))
                o_ref[...] += d[0:tm]
                if rows > tm:
                    os_ref[...] += d[tm:rows]
            return carry

        lax.fori_loop(0, n_trips, trip, 0)

    @pl.when(i == 0)
    def _():
        walk(tm + n_small)

    @pl.when(i > 0)
    def _():
        walk(tm)

    @pl.when(i == last)
    def _():
        x_copy(last).wait()
        for c in w_copies(0, lax.rem((last + 1) * n_trips, FFN_SLOTS)):
            c.wait()


def _ffn(x, xs, g, wg, wu, wd, *, layer, half, tm, tf):
    m, d = x.shape
    n_small = xs.shape[0]
    f = wg.shape[-1]
    assert m % tm == 0 and f % (FFN_TRIP * tf) == 0
    rows = lambda i: (i, 0)
    const = lambda i: (0, 0)
    hbm = pl.BlockSpec(memory_space=pl.ANY)
    return pl.pallas_call(
        functools.partial(_ffn_kernel, layer=layer, half=half, tm=tm, tf=tf),
        grid=(m // tm,),
        in_specs=[hbm, pl.BlockSpec((n_small, d), const), pl.BlockSpec((1, d), const),
                  hbm, hbm, hbm],
        out_specs=[pl.BlockSpec((tm, d), rows), pl.BlockSpec((n_small, d), const)],
        out_shape=[jax.ShapeDtypeStruct((m, d), F32), jax.ShapeDtypeStruct((n_small, d), F32)],
        scratch_shapes=[
            pltpu.VMEM((tm, d), F32),
            pltpu.VMEM((tm + n_small, d), BF16),
            pltpu.VMEM((FFN_SLOTS, d, FFN_TRIP * tf), F32),
            pltpu.VMEM((FFN_SLOTS, d, FFN_TRIP * tf), F32),
            pltpu.VMEM((FFN_SLOTS, FFN_TRIP * tf, d), F32),
            pltpu.SemaphoreType.DMA((1,)),
            pltpu.SemaphoreType.DMA((3, FFN_SLOTS)),
        ],
        compiler_params=_params(("arbitrary",)),
        name="ffn",
    )(x, xs, g.reshape(1, d), wg, wu, wd)


def _norm_matmul_kernel(x_ref, g_ref, w_ref, o_ref, h_ref):
    @pl.when(pl.program_id(1) == 0)
    def _():
        h_ref[...] = _rms(x_ref[...], g_ref[...]).astype(BF16)

    o_ref[...] = _dot(h_ref[...], w_ref[...])


def _norm_matmul(x, g, w, *, tm, tn):
    m, d = x.shape
    n = w.shape[1]
    return pl.pallas_call(
        _norm_matmul_kernel,
        grid=(m // tm, n // tn),
        in_specs=[
            pl.BlockSpec((tm, d), lambda i, j: (i, 0)),
            pl.BlockSpec((1, d), lambda i, j: (0, 0)),
            pl.BlockSpec((d, tn), lambda i, j: (0, j)),
        ],
        out_specs=pl.BlockSpec((tm, tn), lambda i, j: (i, j)),
        out_shape=jax.ShapeDtypeStruct((m, n), F32),
        scratch_shapes=[pltpu.VMEM((tm, d), BF16)],
        compiler_params=_params(("parallel", "arbitrary")),
        name="norm_matmul",
    )(x, g.reshape(1, d), w)


def _head_norm(p, g):
    cols = []
    for h in range(N_HEADS):
        cols.append(_rms(p[:, h * HEAD_DIM:(h + 1) * HEAD_DIM], g))
    return jnp.concatenate(cols, axis=-1)


def _even_in_kernel(x_ref, g_ref, w_ref, qg_ref, kg_ref, avg_ref, wmix_ref, bmix_ref,
                    q_ref, k_ref, v_ref, gated_ref, *rest, rows):
    *maybe_avn_ref, h_ref = rest
    sec = q_ref.shape[-1]
    h_ref[...] = _rms(x_ref[...], g_ref[...]).astype(BF16)

    def section(s):
        return _dot(h_ref[...], w_ref[:, s * sec:(s + 1) * sec])

    q_ref[...] = _head_norm(section(0), qg_ref[...]).astype(BF16)
    k_ref[...] = _head_norm(section(1), kg_ref[...])
    v_ref[...] = section(2)
    au = jax.nn.gelu(section(3))
    avn = _rms(jax.nn.gelu(section(4)), avg_ref[...])
    if maybe_avn_ref:
        maybe_avn_ref[0][...] = avn
    avb = avn.astype(BF16)
    k_pad = wmix_ref.shape[2] - rows
    for c in range(avn.shape[0] // rows):
        rs = slice(c * rows, (c + 1) * rows)
        for grp in range(N_HEADS):
            cs = slice(grp * HEAD_DIM, (grp + 1) * HEAD_DIM)
            blk = avb[rs, cs]
            if k_pad:
                blk = jnp.concatenate([blk, jnp.zeros((k_pad, HEAD_DIM), BF16)], axis=0)
            mix = _dot(wmix_ref[grp], blk) + bmix_ref[:, cs]
            gated_ref[rs, cs] = (au[rs, cs] * mix).astype(BF16)


def _even_in(x, g, w, qg, kg, avg, wmix, bmix, *, tm, want_avn):
    m, d = x.shape
    rows = wmix.shape[1]
    assert tm % rows == 0
    sec = N_HEADS * HEAD_DIM
    row_blk = lambda i: (i, 0)
    const = lambda i: (0, 0)
    once = dict(pipeline_mode=pl.Buffered(1))
    out_blk = pl.BlockSpec((tm, sec), row_blk)
    return pl.pallas_call(
        functools.partial(_even_in_kernel, rows=rows),
        grid=(m // tm,),
        in_specs=[
            pl.BlockSpec((tm, d), row_blk),
            pl.BlockSpec((1, d), const),
            pl.BlockSpec((d, 5 * sec), const, **once),
            pl.BlockSpec((1, HEAD_DIM), const),
            pl.BlockSpec((1, HEAD_DIM), const),
            pl.BlockSpec((1, sec), const),
            pl.BlockSpec(wmix.shape, lambda i: (0, 0, 0)),
            pl.BlockSpec((rows, sec), const),
        ],
        out_specs=[out_blk] * (5 if want_avn else 4),
        out_shape=[
            jax.ShapeDtypeStruct((m, sec), BF16),
            jax.ShapeDtypeStruct((m, sec), F32),
            jax.ShapeDtypeStruct((m, sec), F32),
            jax.ShapeDtypeStruct((m, sec), BF16),
        ] + [jax.ShapeDtypeStruct((m, sec), F32)] * want_avn,
        scratch_shapes=[pltpu.VMEM((tm, d), BF16)],
        compiler_params=_params(("parallel",)),
        name="even_in",
    )(x, g.reshape(1, d), w, qg.reshape(1, HEAD_DIM), kg.reshape(1, HEAD_DIM),
      avg.reshape(1, sec), wmix, bmix)


def _softplus(z):
    return jnp.maximum(z, 0.0) + jnp.log(1.0 + jnp.exp(-jnp.abs(z)))


def _suffix_sum_weights():
    row = jnp.arange(2 * PAGE)[:, None] % PAGE
    col = jnp.arange(2 * PAGE)[None, :]
    return -jnp.where(col < PAGE, row > col, True).astype(BF16)


def _sb_logits(qs, kbs, biases):
    return [lax.dot_general(q, kb, (((1,), (1,)), ((), ())), preferred_element_type=F32)
            * SB_SCALE + bias for q, kb, bias in zip(qs, kbs, biases)]


def _sb_weights(zs, cum_w, runs, mask, side_work=()):
    rows = zs[0].shape[0]
    sps, parts = [], []
    for c, z in enumerate(zs):
        sp = _softplus(z)
        sps.append(sp)
        sp = sp if mask is None else jnp.where(mask, sp, 0.0)
        hi = sp.astype(BF16)
        lo = (sp - hi.astype(F32)).astype(BF16)
        parts.append(jnp.concatenate([hi, lo], axis=1))
        if side_work:
            side_work[c]()
    cum = _dot(jnp.concatenate(parts, axis=0), cum_w)
    weights, new_runs, run = [], [], None
    for c, (z, sp) in enumerate(zip(zs, sps)):
        run = run if runs[c] is None else runs[c]
        cum_c = cum[c * rows:(c + 1) * rows]
        a = jnp.exp(z - sp + cum_c[:, :PAGE] + run)
        if mask is not None:
            a = jnp.where(mask, a, 0.0)
        weights.append(a.astype(BF16))
        run = run + cum_c[:, PAGE:]
        new_runs.append(run)
    return weights, new_runs


def _sb_prompt_kernel(bias_ref, q_ref, k_ref, v_ref, cw_ref, o_ref,
                      kb_ref, vb_ref, acc_ref, run_ref, *, heads):
    hb = pl.program_id(1)
    i = pl.program_id(2)

    @pl.when(i == 0)
    def _():
        kb_ref[...] = k_ref[...].astype(BF16)
        vb_ref[...] = v_ref[...].astype(BF16)

    cum_w = cw_ref[...]
    cols = [slice(hh * HEAD_DIM, (hh + 1) * HEAD_DIM) for hh in range(heads)]
    qs = [q_ref[:, cs] for cs in cols]
    biases = [bias_ref[hb * heads + hh] for hh in range(heads)]

    def logits(block):
        start = pl.multiple_of(block * PAGE, PAGE)
        return _sb_logits(qs, [kb_ref[pl.ds(start, PAGE), cs] for cs in cols], biases)

    def weights(zs, mask, side_work=()):
        a, runs = _sb_weights(zs, cum_w, [run_ref[:, cs] for cs in cols], mask, side_work)
        for run, cs in zip(runs, cols):
            run_ref[:, cs] = run
        return a

    def accumulate(a, block):
        start = pl.multiple_of(block * PAGE, PAGE)
        for a_c, cs in zip(a, cols):
            acc_ref[:, cs] += _dot(a_c, vb_ref[pl.ds(start, PAGE), cs])

    acc_ref[...] = jnp.zeros_like(acc_ref)
    run_ref[...] = jnp.zeros_like(run_ref)
    below_diag = (lax.broadcasted_iota(jnp.int32, (PAGE, PAGE), 1)
                  < lax.broadcasted_iota(jnp.int32, (PAGE, PAGE), 0))
    a_diag = weights(logits(i), below_diag)

    def body(it, carry):
        zs, a_prev = carry
        block = i - 1 - it
        nxt = pl.multiple_of(jnp.maximum(block - 1, 0) * PAGE, PAGE)
        prev = pl.multiple_of((block + 1) * PAGE, PAGE)
        zs_next = [None] * heads

        def side(c):
            def work():
                cs = cols[c]
                zs_next[c], = _sb_logits([qs[c]], [kb_ref[pl.ds(nxt, PAGE), cs]], [biases[c]])
                acc_ref[:, cs] += _dot(a_prev[c], vb_ref[pl.ds(prev, PAGE), cs])
            return work

        a_cur = weights(zs, None, [side(c) for c in range(heads)])
        return tuple(zs_next), tuple(a_cur)

    _, a_last = lax.fori_loop(0, i, body, (tuple(logits(jnp.maximum(i - 1, 0))), tuple(a_diag)))
    accumulate(a_last, 0)
    o_ref[...] = acc_ref[...].astype(BF16)


def _sb_prompt(q, k, v, bias, *, batch, heads):
    m, width = q.shape
    t = m // batch
    nq = t // PAGE
    hw = heads * HEAD_DIM
    qo_spec = pl.BlockSpec((PAGE, hw), lambda b, h, i: (b * nq + i, h))
    kv_spec = pl.BlockSpec((t, hw), lambda b, h, i: (b, h))
    return pl.pallas_call(
        functools.partial(_sb_prompt_kernel, heads=heads),
        grid=(batch, width // hw, nq),
        in_specs=[pl.BlockSpec(memory_space=pltpu.SMEM), qo_spec, kv_spec, kv_spec,
                  pl.BlockSpec((2 * PAGE, 2 * PAGE), lambda b, h, i: (0, 0))],
        out_specs=qo_spec,
        out_shape=jax.ShapeDtypeStruct((m, width), BF16),
        scratch_shapes=[pltpu.VMEM((t, hw), BF16), pltpu.VMEM((t, hw), BF16),
                        pltpu.VMEM((PAGE, hw), F32), pltpu.VMEM((PAGE, hw), F32)],
        compiler_params=_params(("parallel", "parallel", "arbitrary")),
        name="sb_prompt",
    )(bias, q, k, v, _suffix_sum_weights())


def _sb_sample_kernel(pt_ref, q_ref, bias_ref, cw_ref, knew_ref, vnew_ref, *rest, tq, pages):
    kpage_refs, vpage_refs = rest[:pages], rest[pages:2 * pages]
    o_ref, acc_ref, run_ref = rest[2 * pages:]
    j = pl.program_id(1)
    q = q_ref[0]
    bias = bias_ref[...]
    cum_w = cw_ref[...]

    @pl.when(j == 0)
    def _():
        t_pos = lax.rem(lax.broadcasted_iota(jnp.int32, (q.shape[0], PAGE), 0), tq)
        s_pos = lax.broadcasted_iota(jnp.int32, (q.shape[0], PAGE), 1)
        def new_block(ref):
            rows_in = ref.shape[1]
            fill = jnp.zeros((PAGE - rows_in, ref.shape[2]), BF16)
            return jnp.concatenate([ref[0].astype(BF16), fill], axis=0)

        zs = _sb_logits([q], [new_block(knew_ref)], [bias])
        a, runs = _sb_weights(zs, cum_w, [jnp.zeros((q.shape[0], PAGE), F32)], s_pos < t_pos)
        acc_ref[...] = _dot(a[0], new_block(vnew_ref))
        run_ref[...] = runs[0]

    @pl.when(j > 0)
    def _():
        def page(ref):
            heads = [ref[0, pl.ds(h, PAGE, stride=N_HEADS), :] for h in range(N_HEADS)]
            return jnp.concatenate(heads, axis=1).astype(BF16)

        zs = _sb_logits([q] * pages, [page(r) for r in kpage_refs], [bias] * pages)
        a, runs = _sb_weights(zs, cum_w, [run_ref[...]] + [None] * (pages - 1), None)
        ds = [_dot(a_p, page(r)) for a_p, r in zip(a, vpage_refs)]
        acc_ref[...] += functools.reduce(lambda x, y: x + y, ds)
        run_ref[...] = runs[-1]

    @pl.when(j == pl.num_programs(1) - 1)
    def _():
        for h in range(N_HEADS):
            cs = slice(h * HEAD_DIM, (h + 1) * HEAD_DIM)
            o_ref[0, :, cs] = acc_ref[h * tq:(h + 1) * tq, cs].astype(BF16)


def _sb_sample(q, k_new, v_new, cache_k, cache_v, page_table, bias, *, batch, page_base, pages):
    m, width = q.shape
    tq = m // batch
    n_pages = page_table.shape[1]
    assert n_pages % pages == 0
    rows = N_HEADS * tq
    q4 = q.reshape(batch, tq, N_HEADS, HEAD_DIM).transpose(0, 2, 1, 3)
    eye = jnp.eye(N_HEADS, dtype=BF16)
    q_rows = (q4[:, :, :, None, :] * eye[None, :, None, :, None]).reshape(batch, rows, width)
    bias_rows = jnp.broadcast_to(jnp.repeat(bias, tq)[:, None], (rows, PAGE))
    new_rows = -(-tq // BF16_ROWS) * BF16_ROWS
    pad = ((0, 0), (0, new_rows - tq), (0, 0))
    k_pad = jnp.pad(k_new.reshape(batch, tq, width), pad)
    v_pad = jnp.pad(v_new.reshape(batch, tq, width), pad)

    def page_spec(p):
        def index(b, j, pt):
            return (page_base + pt[b, n_pages - 1 - (jnp.maximum(j, 1) - 1) * pages - p], 0, 0)
        return pl.BlockSpec((1, PAGE * N_HEADS, HEAD_DIM), index)

    per_b = lambda b, j, pt: (b, 0, 0)
    const = lambda b, j, pt: (0, 0)
    new_spec = pl.BlockSpec((1, new_rows, width), per_b)
    page_specs = [page_spec(p) for p in range(pages)]
    out = pl.pallas_call(
        functools.partial(_sb_sample_kernel, tq=tq, pages=pages),
        grid_spec=pltpu.PrefetchScalarGridSpec(
            num_scalar_prefetch=1,
            grid=(batch, n_pages // pages + 1),
            in_specs=[
                pl.BlockSpec((1, rows, width), per_b),
                pl.BlockSpec((rows, PAGE), const),
                pl.BlockSpec((2 * PAGE, 2 * PAGE), const),
                new_spec, new_spec, *page_specs, *page_specs,
            ],
            out_specs=pl.BlockSpec((1, tq, width), per_b),
            scratch_shapes=[pltpu.VMEM((rows, width), F32), pltpu.VMEM((rows, PAGE), F32)],
        ),
        out_shape=jax.ShapeDtypeStruct((batch, tq, width), BF16),
        compiler_params=_params(("parallel", "arbitrary")),
        name="sb_sample",
    )(page_table, q_rows, bias_rows, _suffix_sum_weights(), k_pad, v_pad,
      *([cache_k] * pages), *([cache_v] * pages))
    return out.reshape(m, width)


def _out_proj_kernel(a_ref, b_ref, wa_ref, wb_ref, r_ref, o_ref):
    o_ref[...] = (r_ref[...] + _dot(a_ref[...], wa_ref[...].astype(BF16))
                  + _dot(b_ref[...], wb_ref[...].astype(BF16)))


def _out_proj(a, b, w, resid, *, tm, tn):
    m, ka = a.shape
    n = w.shape[1]
    assert b.shape[1] == ka and w.shape[0] == 2 * ka
    once = dict(pipeline_mode=pl.Buffered(1)) if tn == n else {}
    return pl.pallas_call(
        _out_proj_kernel,
        grid=(m // tm, n // tn),
        in_specs=[
            pl.BlockSpec((tm, ka), lambda i, j: (i, 0)),
            pl.BlockSpec((tm, ka), lambda i, j: (i, 0)),
            pl.BlockSpec((ka, tn), lambda i, j: (0, j), **once),
            pl.BlockSpec((ka, tn), lambda i, j: (1, j), **once),
            pl.BlockSpec((tm, tn), lambda i, j: (i, j)),
        ],
        out_specs=pl.BlockSpec((tm, tn), lambda i, j: (i, j)),
        out_shape=jax.ShapeDtypeStruct((m, n), F32),
        compiler_params=_params(("parallel", "parallel")),
        name="out_proj",
    )(a, b, w, w, resid)


def _odd_mix_kernel(*refs, tt, pos0, project):
    n_in = 3 if project else 1
    (pool_prev_ref, conv_prev_ref, cwg_ref, cscale_ref, convw_ref,
     c_ref, d_ref, pool_out_ref, conv_out_ref, xbuf_ref, ubuf_ref) = refs[n_in:]
    t_idx = pl.program_id(1)
    width = c_ref.shape[-1]

    @pl.when(t_idx == 0)
    def _():
        xbuf_ref[HIST_PAD - POOL_HIST:HIST_PAD, :] = pool_prev_ref[0]
        ubuf_ref[HIST_PAD - CONV_HIST:HIST_PAD, :] = conv_prev_ref[0]

    if project:
        x_ref, g_ref, w_ref = refs[:n_in]
        h = _rms(x_ref[0], g_ref[...]).astype(BF16)
        section = lambda s: _dot(h, w_ref[:, s * width:(s + 1) * width])
    else:
        p_ref, = refs[:n_in]
        section = lambda s: p_ref[0, :, s * width:(s + 1) * width]
    xc = section(0)
    gb = section(1)
    u = section(2) * section(3)
    xbuf_ref[HIST_PAD:HIST_PAD + tt, :] = xc
    ubuf_ref[HIST_PAD:HIST_PAD + tt, :] = u

    seen = pos0 + t_idx * tt + lax.broadcasted_iota(jnp.int32, (tt, 1), 0) + 1
    for grp, win in enumerate(POOL_WINDOWS):
        cs = slice(grp * POOL_GROUP, (grp + 1) * POOL_GROUP)
        s = xc[:, cs]
        for back in range(1, win):
            s = s + xbuf_ref[HIST_PAD - back:HIST_PAD - back + tt, cs]
        count = jnp.minimum(seen, win).astype(F32)
        pooled = s / count - xc[:, cs]
        mixed = _dot(pooled.astype(BF16), cwg_ref[grp]) * cscale_ref[:, cs]
        c_ref[0, :, cs] = mixed.astype(BF16)

    conv = (ubuf_ref[HIST_PAD - 2:HIST_PAD - 2 + tt, :] * convw_ref[0:1, :]
            + ubuf_ref[HIST_PAD - 1:HIST_PAD - 1 + tt, :] * convw_ref[1:2, :]
            + u * convw_ref[2:3, :])
    d_ref[0] = (gb * conv).astype(BF16)

    pool_tail = xbuf_ref[HIST_PAD + tt - POOL_HIST:HIST_PAD + tt, :]
    conv_tail = ubuf_ref[HIST_PAD + tt - CONV_HIST:HIST_PAD + tt, :]
    pool_out_ref[0] = pool_tail
    conv_out_ref[0] = conv_tail
    xbuf_ref[HIST_PAD - POOL_HIST:HIST_PAD, :] = pool_tail
    ubuf_ref[HIST_PAD - CONV_HIST:HIST_PAD, :] = conv_tail


def _odd_mix(rows_in, pool_prev, conv_prev, cwg, cscale, convw, *, tt, pos0):
    batch, t, _ = rows_in[0].shape
    width = pool_prev.shape[-1]
    seq_blk = lambda b, i: (b, i, 0)
    per_b = lambda b, i: (b, 0, 0)
    const = lambda b, i: (0, 0)
    project = len(rows_in) == 3
    if project:
        x, g, w = rows_in
        d = x.shape[-1]
        rows_in = (x, g.reshape(1, d), w)
        rows_specs = [pl.BlockSpec((1, tt, d), seq_blk), pl.BlockSpec((1, d), const),
                      pl.BlockSpec(w.shape, const, pipeline_mode=pl.Buffered(1))]
    else:
        rows_specs = [pl.BlockSpec((1, tt, 4 * width), seq_blk)]
    return pl.pallas_call(
        functools.partial(_odd_mix_kernel, tt=tt, pos0=pos0, project=project),
        grid=(batch, t // tt),
        in_specs=rows_specs + [
            pl.BlockSpec((1, POOL_HIST, width), per_b),
            pl.BlockSpec((1, CONV_HIST, width), per_b),
            pl.BlockSpec(cwg.shape, lambda b, i: (0, 0, 0)),
            pl.BlockSpec((1, width), lambda b, i: (0, 0)),
            pl.BlockSpec(convw.shape, lambda b, i: (0, 0)),
        ],
        out_specs=[
            pl.BlockSpec((1, tt, width), seq_blk),
            pl.BlockSpec((1, tt, width), seq_blk),
            pl.BlockSpec((1, POOL_HIST, width), per_b),
            pl.BlockSpec((1, CONV_HIST, width), per_b),
        ],
        out_shape=[
            jax.ShapeDtypeStruct((batch, t, width), BF16),
            jax.ShapeDtypeStruct((batch, t, width), BF16),
            jax.ShapeDtypeStruct((batch, POOL_HIST, width), F32),
            jax.ShapeDtypeStruct((batch, CONV_HIST, width), F32),
        ],
        scratch_shapes=[pltpu.VMEM((HIST_PAD + tt, width), F32),
                        pltpu.VMEM((HIST_PAD + tt, width), F32)],
        compiler_params=_params(("parallel", "arbitrary")),
        name="odd_mix",
    )(*rows_in, pool_prev, conv_prev, cwg, cscale.reshape(1, width), convw)


def _pick_tile(m, target):
    return target if m % target == 0 else m


def kernel(x_prompt, x_sample, cache_k, cache_v, state_pool, state_conv, page_table, norm_g, ffn_w_gate, ffn_w_up, ffn_w_down, w_in_even, w_out_even, q_norm_g, k_norm_g, sb_bias, a_v_norm_g, a_ws, a_bs, w_in_odd, w_out_odd, c_wg, c_scale, d_conv_w):
    bp, tp, d = x_prompt.shape
    bs, ts, _ = x_sample.shape
    depth = norm_g.shape[0]
    mp, ms = bp * tp, bs * ts
    sec = N_HEADS * HEAD_DIM

    tm_p = _pick_tile(mp, ROW_TILE)
    n_out = w_out_even.shape[-1]

    def ffn_pair(yp, ys, layer, half):
        return _ffn(yp, ys, norm_g[layer, 2 * half], ffn_w_gate, ffn_w_up, ffn_w_down,
                    layer=layer, half=half, tm=_pick_tile(mp, FFN_ROWS),
                    tf=_pick_tile(ffn_w_gate.shape[-1], FFN_COLS))

    yp = x_prompt.reshape(mp, d)
    ys = x_sample.reshape(ms, d)
    outs = {name: [] for name in ("kp", "vp", "ks", "vs", "av", "pp", "ps", "cp", "cs")}

    for layer in range(depth):
        i = layer // 2
        yp, ys = ffn_pair(yp, ys, layer, 0)
        if layer % 2 == 0:
            w_in_b, w_out = w_in_even[i].astype(BF16), w_out_even[i]
            tril_p = jnp.tril(a_ws[i]).astype(BF16)
            bias_p = jnp.repeat(a_bs[i].T, HEAD_DIM, axis=1)
            tril_s = jnp.tril(a_ws[i][:, :ts, :ts])
            eye = jnp.eye(bs, dtype=F32)
            wmix_s = (eye[None, :, None, :, None] * tril_s[:, None, :, None, :]).reshape(
                N_HEADS, ms, ms).astype(BF16)
            wmix_s = jnp.pad(wmix_s, ((0, 0), (0, 0), (0, max(0, HEAD_DIM - ms))))
            bias_s = jnp.tile(jnp.repeat(a_bs[i][:, :ts].T, HEAD_DIM, axis=1), (bs, 1))

            q, k, v, gated = _even_in(yp, norm_g[layer, 1], w_in_b, q_norm_g[i],
                                      k_norm_g[i], a_v_norm_g[i], tril_p, bias_p,
                                      tm=tm_p, want_avn=False)
            sb = _sb_prompt(q, k, v, sb_bias[i], batch=bp, heads=HEADS_PER_STEP)
            yp = _out_proj(gated, sb, w_out, yp, tm=tm_p, tn=n_out)
            outs["kp"].append(k.reshape(bp, tp, N_HEADS, HEAD_DIM))
            outs["vp"].append(v.reshape(bp, tp, N_HEADS, HEAD_DIM))

            q, k, v, gated, avn = _even_in(ys, norm_g[layer, 1], w_in_b, q_norm_g[i],
                                           k_norm_g[i], a_v_norm_g[i], wmix_s, bias_s, tm=ms,
                                           want_avn=True)
            n_phys = cache_k.shape[1]
            pool_shape = (cache_k.shape[0] * n_phys, PAGE * N_HEADS, HEAD_DIM)
            sb = _sb_sample(q, k, v, cache_k.reshape(pool_shape), cache_v.reshape(pool_shape),
                            page_table, sb_bias[i], batch=bs, page_base=i * n_phys,
                            pages=PAGES_PER_STEP)
            ys = _out_proj(gated, sb, w_out, ys, tm=ms, tn=n_out)
            outs["ks"].append(k.reshape(bs, ts, N_HEADS, HEAD_DIM))
            outs["vs"].append(v.reshape(bs, ts, N_HEADS, HEAD_DIM))
            outs["av"].append(avn.reshape(bs, ts, sec))
        else:
            width = c_scale.shape[-1]
            w_in_b, w_out = w_in_odd[i].astype(BF16), w_out_odd[i]
            cwg_b = c_wg[i].astype(BF16)
            c_out, d_out, pool_st, conv_st = _odd_mix(
                (yp.reshape(bp, tp, d), norm_g[layer, 1], w_in_b),
                jnp.zeros((bp, POOL_HIST, width), F32), jnp.zeros((bp, CONV_HIST, width), F32),
                cwg_b, c_scale[i], d_conv_w[i], tt=_pick_tile(tp, ROW_TILE), pos0=0)
            yp = _out_proj(c_out.reshape(mp, width), d_out.reshape(mp, width), w_out,
                           yp, tm=tm_p, tn=n_out)
            outs["pp"].append(pool_st)
            outs["cp"].append(conv_st)

            p = _norm_matmul(ys, norm_g[layer, 1], w_in_b, tm=ms, tn=width)
            past = page_table.shape[1] * PAGE
            c_out, d_out, pool_st, conv_st = _odd_mix(
                (p.reshape(bs, ts, 4 * width),), state_pool[i], state_conv[i],
                cwg_b, c_scale[i], d_conv_w[i], tt=ts, pos0=past)
            ys = _out_proj(c_out.reshape(ms, width), d_out.reshape(ms, width), w_out,
                           ys, tm=ms, tn=n_out)
            outs["ps"].append(pool_st)
            outs["cs"].append(conv_st)
        yp, ys = ffn_pair(yp, ys, layer, 1)

    return (yp.reshape(bp, tp, d), ys.reshape(bs, ts, d),
            jnp.stack(outs["kp"]), jnp.stack(outs["vp"]),
            jnp.stack(outs["ks"]), jnp.stack(outs["vs"]), jnp.stack(outs["av"]),
            jnp.stack(outs["pp"]), jnp.stack(outs["ps"]),
            jnp.stack(outs["cp"]), jnp.stack(outs["cs"]))
```

```python
import functools

import jax
import jax.numpy as jnp
from jax import lax
from jax.experimental import pallas as pl
from jax.experimental.pallas import tpu as pltpu

F32 = jnp.float32
BF16 = jnp.bfloat16

EPS = 1e-6
HEAD_DIM = 128
N_HEADS = 8
SB_SCALE = HEAD_DIM ** -0.5
PAGE = 128
POOL_WINDOWS = (2, 4, 8, 16)
POOL_GROUP = 256
POOL_HIST = 15
CONV_HIST = 2
HIST_PAD = 16
BF16_ROWS = 16
VMEM_LIMIT = 56 * 1024 * 1024
FFN_ROWS = 1024
FFN_COLS = 256
ROW_TILE = 512
HEADS_PER_STEP = 8
HEAD_GROUP = 4
PAGES_PER_STEP = 16
FFN_TRIP = 2
FFN_SLOTS = 2


def _params(sem):
    return pltpu.CompilerParams(dimension_semantics=sem, vmem_limit_bytes=VMEM_LIMIT)


def _rms(x, g):
    return x * lax.rsqrt(jnp.mean(x * x, axis=-1, keepdims=True) + EPS) * g


def _dot(a, b):
    return jnp.dot(a, b, preferred_element_type=F32)


def _ffn_kernel(x_hbm, xs_ref, g_ref, wg_hbm, wu_hbm, wd_hbm, o_ref, os_ref,
                x_buf, h_ref, wg_buf, wu_buf, wd_buf, x_sem, w_sem, *, layer, half, tm, tf):
    i = pl.program_id(0)
    last = pl.num_programs(0) - 1
    n_tiles = wg_hbm.shape[-1] // tf
    n_small = xs_ref.shape[0]

    def x_copy(tile):
        return pltpu.make_async_copy(x_hbm.at[pl.ds(tile * tm, tm), :], x_buf, x_sem.at[0])

    n_trips = n_tiles // FFN_TRIP

    def w_copies(trip_idx, slot):
        cols = pl.ds(trip_idx * FFN_TRIP * tf, FFN_TRIP * tf)
        return (
            pltpu.make_async_copy(wg_hbm.at[layer, half, :, cols], wg_buf.at[slot], w_sem.at[0, slot]),
            pltpu.make_async_copy(wu_hbm.at[layer, half, :, cols], wu_buf.at[slot], w_sem.at[1, slot]),
            pltpu.make_async_copy(wd_hbm.at[layer, half, cols, :], wd_buf.at[slot], w_sem.at[2, slot]),
        )

    @pl.when(i == 0)
    def _():
        x_copy(0).start()
        for c in w_copies(0, 0):
            c.start()
        xs = xs_ref[...]
        h_ref[tm:tm + n_small, :] = _rms(xs, g_ref[...]).astype(BF16)
        os_ref[...] = xs

    x_copy(i).wait()
    x = x_buf[...]
    h_ref[0:tm, :] = _rms(x, g_ref[...]).astype(BF16)
    o_ref[...] = x
    x_copy(jnp.minimum(i + 1, last)).start()

    def walk(rows):
        def trip(jj, carry):
            step = i * n_trips + jj
            slot = lax.rem(step, FFN_SLOTS)
            for c in w_copies(jj, slot):
                c.wait()
            for c in w_copies(lax.rem(jj + 1, n_trips), lax.rem(step + 1, FFN_SLOTS)):
                c.start()
            for u in range(FFN_TRIP):
                cols = slice(u * tf, (u + 1) * tf)
                h = h_ref[0:rows, :]
                gate = _dot(h, wg_buf[slot, :, cols].astype(BF16))
                up = _dot(h, wu_buf[slot, :, cols].astype(BF16))
                act = (gate * jax.nn.sigmoid(gate) * up).astype(BF16)
                d = 0.5 * _dot(act, wd_buf[slot, cols, :].astype(BF16))
                o_ref[...] += d[0:tm]
                if rows > tm:
                    os_ref[...] += d[tm:rows]
            return carry

        lax.fori_loop(0, n_trips, trip, 0)

    @pl.when(i == 0)
    def _():
        walk(tm + n_small)

    @pl.when(i > 0)
    def _():
        walk(tm)

    @pl.when(i == last)
    def _():
        x_copy(last).wait()
        for c in w_copies(0, lax.rem((last + 1) * n_trips, FFN_SLOTS)):
            c.wait()


def _ffn(x, xs, g, wg, wu, wd, *, layer, half, tm, tf):
    m, d = x.shape
    n_small = xs.shape[0]
    f = wg.shape[-1]
    assert m % tm == 0 and f % (FFN_TRIP * tf) == 0
    rows = lambda i: (i, 0)
    const = lambda i: (0, 0)
    hbm = pl.BlockSpec(memory_space=pl.ANY)
    return pl.pallas_call(
        functools.partial(_ffn_kernel, layer=layer, half=half, tm=tm, tf=tf),
        grid=(m // tm,),
        in_specs=[hbm, pl.BlockSpec((n_small, d), const), pl.BlockSpec((1, d), const),
                  hbm, hbm, hbm],
        out_specs=[pl.BlockSpec((tm, d), rows), pl.BlockSpec((n_small, d), const)],
        out_shape=[jax.ShapeDtypeStruct((m, d), F32), jax.ShapeDtypeStruct((n_small, d), F32)],
        scratch_shapes=[
            pltpu.VMEM((tm, d), F32),
            pltpu.VMEM((tm + n_small, d), BF16),
            pltpu.VMEM((FFN_SLOTS, d, FFN_TRIP * tf), F32),
            pltpu.VMEM((FFN_SLOTS, d, FFN_TRIP * tf), F32),
            pltpu.VMEM((FFN_SLOTS, FFN_TRIP * tf, d), F32),
            pltpu.SemaphoreType.DMA((1,)),
            pltpu.SemaphoreType.DMA((3, FFN_SLOTS)),
        ],
        compiler_params=_params(("arbitrary",)),
        name="ffn",
    )(x, xs, g.reshape(1, d), wg, wu, wd)


def _norm_matmul_kernel(x_ref, g_ref, w_ref, o_ref, h_ref):
    @pl.when(pl.program_id(1) == 0)
    def _():
        h_ref[...] = _rms(x_ref[...], g_ref[...]).astype(BF16)

    o_ref[...] = _dot(h_ref[...], w_ref[...])


def _norm_matmul(x, g, w, *, tm, tn):
    m, d = x.shape
    n = w.shape[1]
    return pl.pallas_call(
        _norm_matmul_kernel,
        grid=(m // tm, n // tn),
        in_specs=[
            pl.BlockSpec((tm, d), lambda i, j: (i, 0)),
            pl.BlockSpec((1, d), lambda i, j: (0, 0)),
            pl.BlockSpec((d, tn), lambda i, j: (0, j)),
        ],
        out_specs=pl.BlockSpec((tm, tn), lambda i, j: (i, j)),
        out_shape=jax.ShapeDtypeStruct((m, n), F32),
        scratch_shapes=[pltpu.VMEM((tm, d), BF16)],
        compiler_params=_params(("parallel", "arbitrary")),
        name="norm_matmul",
    )(x, g.reshape(1, d), w)


def _head_norm(p, g):
    cols = []
    for h in range(N_HEADS):
        cols.append(_rms(p[:, h * HEAD_DIM:(h + 1) * HEAD_DIM], g))
    return jnp.concatenate(cols, axis=-1)


def _even_in_kernel(x_ref, g_ref, w_ref, qg_ref, kg_ref, avg_ref, wmix_ref, bmix_ref,
                    q_ref, k_ref, v_ref, gated_ref, *rest, rows):
    *maybe_avn_ref, h_ref = rest
    sec = q_ref.shape[-1]
    h_ref[...] = _rms(x_ref[...], g_ref[...]).astype(BF16)

    def section(s):
        return _dot(h_ref[...], w_ref[:, s * sec:(s + 1) * sec])

    q_ref[...] = _head_norm(section(0), qg_ref[...]).astype(BF16)
    k_ref[...] = _head_norm(section(1), kg_ref[...])
    v_ref[...] = section(2)
    au = jax.nn.gelu(section(3))
    avn = _rms(jax.nn.gelu(section(4)), avg_ref[...])
    if maybe_avn_ref:
        maybe_avn_ref[0][...] = avn
    avb = avn.astype(BF16)
    k_pad = wmix_ref.shape[2] - rows
    for c in range(avn.shape[0] // rows):
        rs = slice(c * rows, (c + 1) * rows)
        for grp in range(N_HEADS):
            cs = slice(grp * HEAD_DIM, (grp + 1) * HEAD_DIM)
            blk = avb[rs, cs]
            if k_pad:
                blk = jnp.concatenate([blk, jnp.zeros((k_pad, HEAD_DIM), BF16)], axis=0)
            mix = _dot(wmix_ref[grp], blk) + bmix_ref[:, cs]
            gated_ref[rs, cs] = (au[rs, cs] * mix).astype(BF16)


def _even_in(x, g, w, qg, kg, avg, wmix, bmix, *, tm, want_avn):
    m, d = x.shape
    rows = wmix.shape[1]
    assert tm % rows == 0
    sec = N_HEADS * HEAD_DIM
    row_blk = lambda i: (i, 0)
    const = lambda i: (0, 0)
    once = dict(pipeline_mode=pl.Buffered(1))
    out_blk = pl.BlockSpec((tm, sec), row_blk)
    return pl.pallas_call(
        functools.partial(_even_in_kernel, rows=rows),
        grid=(m // tm,),
        in_specs=[
            pl.BlockSpec((tm, d), row_blk),
            pl.BlockSpec((1, d), const),
            pl.BlockSpec((d, 5 * sec), const, **once),
            pl.BlockSpec((1, HEAD_DIM), const),
            pl.BlockSpec((1, HEAD_DIM), const),
            pl.BlockSpec((1, sec), const),
            pl.BlockSpec(wmix.shape, lambda i: (0, 0, 0)),
            pl.BlockSpec((rows, sec), const),
        ],
        out_specs=[out_blk] * (5 if want_avn else 4),
        out_shape=[
            jax.ShapeDtypeStruct((m, sec), BF16),
            jax.ShapeDtypeStruct((m, sec), F32),
            jax.ShapeDtypeStruct((m, sec), F32),
            jax.ShapeDtypeStruct((m, sec), BF16),
        ] + [jax.ShapeDtypeStruct((m, sec), F32)] * want_avn,
        scratch_shapes=[pltpu.VMEM((tm, d), BF16)],
        compiler_params=_params(("parallel",)),
        name="even_in",
    )(x, g.reshape(1, d), w, qg.reshape(1, HEAD_DIM), kg.reshape(1, HEAD_DIM),
      avg.reshape(1, sec), wmix, bmix)


def _softplus(z):
    return jnp.maximum(z, 0.0) + jnp.log(1.0 + jnp.exp(-jnp.abs(z)))


def _suffix_sum_weights():
    row = jnp.arange(2 * PAGE)[:, None] % PAGE
    col = jnp.arange(2 * PAGE)[None, :]
    return -jnp.where(col < PAGE, row > col, True).astype(BF16)


def _sb_logits(qs, kbs, biases):
    return [lax.dot_general(q, kb, (((1,), (1,)), ((), ())), preferred_element_type=F32)
            * SB_SCALE + bias for q, kb, bias in zip(qs, kbs, biases)]


def _sb_weights(zs, cum_w, runs, mask, side_work=()):
    rows = zs[0].shape[0]
    sps, parts = [], []
    for c, z in enumerate(zs):
        sp = _softplus(z)
        sps.append(sp)
        sp = sp if mask is None else jnp.where(mask, sp, 0.0)
        hi = sp.astype(BF16)
        lo = (sp - hi.astype(F32)).astype(BF16)
        parts.append(jnp.concatenate([hi, lo], axis=1))
        if side_work:
            side_work[c]()
    cum = _dot(jnp.concatenate(parts, axis=0), cum_w)
    weights, new_runs, run = [], [], None
    for c, (z, sp) in enumerate(zip(zs, sps)):
        run = run if runs[c] is None else runs[c]
        cum_c = cum[c * rows:(c + 1) * rows]
        a = jnp.exp(z - sp + cum_c[:, :PAGE] + run)
        if mask is not None:
            a = jnp.where(mask, a, 0.0)
        weights.append(a.astype(BF16))
        run = run + cum_c[:, PAGE:]
        new_runs.append(run)
    return weights, new_runs


def _sb_prompt_kernel(bias_ref, q_ref, k_ref, v_ref, cw_ref, o_ref,
                      kb_ref, vb_ref, acc_ref, run_ref, *, heads):
    hb = pl.program_id(1)
    i = pl.program_id(2)

    @pl.when(i == 0)
    def _():
        kb_ref[...] = k_ref[...].astype(BF16)
        vb_ref[...] = v_ref[...].astype(BF16)

    cum_w = cw_ref[...]
    cols = [slice(hh * HEAD_DIM, (hh + 1) * HEAD_DIM) for hh in range(heads)]
    qs = [q_ref[:, cs] for cs in cols]
    biases = [bias_ref[hb * heads + hh] for hh in range(heads)]

    def logits(block):
        start = pl.multiple_of(block * PAGE, PAGE)
        return _sb_logits(qs, [kb_ref[pl.ds(start, PAGE), cs] for cs in cols], biases)

    def weights(zs, mask, side_work=()):
        a = []
        for lo in range(0, heads, HEAD_GROUP):
            grp = slice(lo, lo + HEAD_GROUP)
            a_g, runs = _sb_weights(zs[grp], cum_w, [run_ref[:, cs] for cs in cols[grp]], mask,
                                    side_work[grp])
            for run, cs in zip(runs, cols[grp]):
                run_ref[:, cs] = run
            a.extend(a_g)
        return a

    def accumulate(a, block):
        start = pl.multiple_of(block * PAGE, PAGE)
        for a_c, cs in zip(a, cols):
            acc_ref[:, cs] += _dot(a_c, vb_ref[pl.ds(start, PAGE), cs])

    acc_ref[...] = jnp.zeros_like(acc_ref)
    run_ref[...] = jnp.zeros_like(run_ref)
    below_diag = (lax.broadcasted_iota(jnp.int32, (PAGE, PAGE), 1)
                  < lax.broadcasted_iota(jnp.int32, (PAGE, PAGE), 0))
    a_diag = weights(logits(i), below_diag)

    def body(it, carry):
        zs, a_prev = carry
        block = i - 1 - it
        nxt = pl.multiple_of(jnp.maximum(block - 1, 0) * PAGE, PAGE)
        prev = pl.multiple_of((block + 1) * PAGE, PAGE)
        zs_next = [None] * heads

        def side(c):
            def work():
                cs = cols[c]
                zs_next[c], = _sb_logits([qs[c]], [kb_ref[pl.ds(nxt, PAGE), cs]], [biases[c]])
                acc_ref[:, cs] += _dot(a_prev[c], vb_ref[pl.ds(prev, PAGE), cs])
            return work

        a_cur = weights(zs, None, [side(c) for c in range(heads)])
        return tuple(zs_next), tuple(a_cur)

    _, a_last = lax.fori_loop(0, i, body, (tuple(logits(jnp.maximum(i - 1, 0))), tuple(a_diag)))
    accumulate(a_last, 0)
    o_ref[...] = acc_ref[...].astype(BF16)


def _sb_prompt(q, k, v, bias, *, batch, heads):
    m, width = q.shape
    t = m // batch
    nq = t // PAGE
    hw = heads * HEAD_DIM
    qo_spec = pl.BlockSpec((PAGE, hw), lambda b, h, i: (b * nq + i, h))
    kv_spec = pl.BlockSpec((t, hw), lambda b, h, i: (b, h))
    return pl.pallas_call(
        functools.partial(_sb_prompt_kernel, heads=heads),
        grid=(batch, width // hw, nq),
        in_specs=[pl.BlockSpec(memory_space=pltpu.SMEM), qo_spec, kv_spec, kv_spec,
                  pl.BlockSpec((2 * PAGE, 2 * PAGE), lambda b, h, i: (0, 0))],
        out_specs=qo_spec,
        out_shape=jax.ShapeDtypeStruct((m, width), BF16),
        scratch_shapes=[pltpu.VMEM((t, hw), BF16), pltpu.VMEM((t, hw), BF16),
                        pltpu.VMEM((PAGE, hw), F32), pltpu.VMEM((PAGE, hw), F32)],
        compiler_params=_params(("parallel", "parallel", "arbitrary")),
        name="sb_prompt",
    )(bias, q, k, v, _suffix_sum_weights())


def _sb_sample_kernel(pt_ref, q_ref, bias_ref, cw_ref, knew_ref, vnew_ref, *rest, tq, pages):
    kpage_refs, vpage_refs = rest[:pages], rest[pages:2 * pages]
    o_ref, acc_ref, run_ref = rest[2 * pages:]
    j = pl.program_id(1)
    q = q_ref[0]
    bias = bias_ref[...]
    cum_w = cw_ref[...]

    @pl.when(j == 0)
    def _():
        t_pos = lax.rem(lax.broadcasted_iota(jnp.int32, (q.shape[0], PAGE), 0), tq)
        s_pos = lax.broadcasted_iota(jnp.int32, (q.shape[0], PAGE), 1)
        def new_block(ref):
            rows_in = ref.shape[1]
            fill = jnp.zeros((PAGE - rows_in, ref.shape[2]), BF16)
            return jnp.concatenate([ref[0].astype(BF16), fill], axis=0)

        zs = _sb_logits([q], [new_block(knew_ref)], [bias])
        a, runs = _sb_weights(zs, cum_w, [jnp.zeros((q.shape[0], PAGE), F32)], s_pos < t_pos)
        acc_ref[...] = _dot(a[0], new_block(vnew_ref))
        run_ref[...] = runs[0]

    @pl.when(j > 0)
    def _():
        def page(ref):
            heads = [ref[0, pl.ds(h, PAGE, stride=N_HEADS), :] for h in range(N_HEADS)]
            return jnp.concatenate(heads, axis=1).astype(BF16)

        zs = _sb_logits([q] * pages, [page(r) for r in kpage_refs], [bias] * pages)
        a, runs = _sb_weights(zs, cum_w, [run_ref[...]] + [None] * (pages - 1), None)
        ds = [_dot(a_p, page(r)) for a_p, r in zip(a, vpage_refs)]
        acc_ref[...] += functools.reduce(lambda x, y: x + y, ds)
        run_ref[...] = runs[-1]

    @pl.when(j == pl.num_programs(1) - 1)
    def _():
        for h in range(N_HEADS):
            cs = slice(h * HEAD_DIM, (h + 1) * HEAD_DIM)
            o_ref[0, :, cs] = acc_ref[h * tq:(h + 1) * tq, cs].astype(BF16)


def _sb_sample(q, k_new, v_new, cache_k, cache_v, page_table, bias, *, batch, page_base, pages):
    m, width = q.shape
    tq = m // batch
    n_pages = page_table.shape[1]
    assert n_pages % pages == 0
    rows = N_HEADS * tq
    q4 = q.reshape(batch, tq, N_HEADS, HEAD_DIM).transpose(0, 2, 1, 3)
    eye = jnp.eye(N_HEADS, dtype=BF16)
    q_rows = (q4[:, :, :, None, :] * eye[None, :, None, :, None]).reshape(batch, rows, width)
    bias_rows = jnp.broadcast_to(jnp.repeat(bias, tq)[:, None], (rows, PAGE))
    new_rows = -(-tq // BF16_ROWS) * BF16_ROWS
    pad = ((0, 0), (0, new_rows - tq), (0, 0))
    k_pad = jnp.pad(k_new.reshape(batch, tq, width), pad)
    v_pad = jnp.pad(v_new.reshape(batch, tq, width), pad)

    def page_spec(p):
        def index(b, j, pt):
            return (page_base + pt[b, n_pages - 1 - (jnp.maximum(j, 1) - 1) * pages - p], 0, 0)
        return pl.BlockSpec((1, PAGE * N_HEADS, HEAD_DIM), index)

    per_b = lambda b, j, pt: (b, 0, 0)
    const = lambda b, j, pt: (0, 0)
    new_spec = pl.BlockSpec((1, new_rows, width), per_b)
    page_specs = [page_spec(p) for p in range(pages)]
    out = pl.pallas_call(
        functools.partial(_sb_sample_kernel, tq=tq, pages=pages),
        grid_spec=pltpu.PrefetchScalarGridSpec(
            num_scalar_prefetch=1,
            grid=(batch, n_pages // pages + 1),
            in_specs=[
                pl.BlockSpec((1, rows, width), per_b),
                pl.BlockSpec((rows, PAGE), const),
                pl.BlockSpec((2 * PAGE, 2 * PAGE), const),
                new_spec, new_spec, *page_specs, *page_specs,
            ],
            out_specs=pl.BlockSpec((1, tq, width), per_b),
            scratch_shapes=[pltpu.VMEM((rows, width), F32), pltpu.VMEM((rows, PAGE), F32)],
        ),
        out_shape=jax.ShapeDtypeStruct((batch, tq, width), BF16),
        compiler_params=_params(("parallel", "arbitrary")),
        name="sb_sample",
    )(page_table, q_rows, bias_rows, _suffix_sum_weights(), k_pad, v_pad,
      *([cache_k] * pages), *([cache_v] * pages))
    return out.reshape(m, width)


def _out_proj_kernel(a_ref, b_ref, wa_ref, wb_ref, r_ref, o_ref):
    o_ref[...] = (r_ref[...] + _dot(a_ref[...], wa_ref[...].astype(BF16))
                  + _dot(b_ref[...], wb_ref[...].astype(BF16)))


def _out_proj(a, b, w, resid, *, tm, tn):
    m, ka = a.shape
    n = w.shape[1]
    assert b.shape[1] == ka and w.shape[0] == 2 * ka
    once = dict(pipeline_mode=pl.Buffered(1)) if tn == n else {}
    return pl.pallas_call(
        _out_proj_kernel,
        grid=(m // tm, n // tn),
        in_specs=[
            pl.BlockSpec((tm, ka), lambda i, j: (i, 0)),
            pl.BlockSpec((tm, ka), lambda i, j: (i, 0)),
            pl.BlockSpec((ka, tn), lambda i, j: (0, j), **once),
            pl.BlockSpec((ka, tn), lambda i, j: (1, j), **once),
            pl.BlockSpec((tm, tn), lambda i, j: (i, j)),
        ],
        out_specs=pl.BlockSpec((tm, tn), lambda i, j: (i, j)),
        out_shape=jax.ShapeDtypeStruct((m, n), F32),
        compiler_params=_params(("parallel", "parallel")),
        name="out_proj",
    )(a, b, w, w, resid)


def _odd_mix_kernel(*refs, tt, pos0, project):
    n_in = 3 if project else 1
    (pool_prev_ref, conv_prev_ref, cwg_ref, cscale_ref, convw_ref,
     c_ref, d_ref, pool_out_ref, conv_out_ref, xbuf_ref, ubuf_ref) = refs[n_in:]
    t_idx = pl.program_id(1)
    width = c_ref.shape[-1]

    @pl.when(t_idx == 0)
    def _():
        xbuf_ref[HIST_PAD - POOL_HIST:HIST_PAD, :] = pool_prev_ref[0]
        ubuf_ref[HIST_PAD - CONV_HIST:HIST_PAD, :] = conv_prev_ref[0]

    if project:
        x_ref, g_ref, w_ref = refs[:n_in]
        h = _rms(x_ref[0], g_ref[...]).astype(BF16)
        section = lambda s: _dot(h, w_ref[:, s * width:(s + 1) * width])
    else:
        p_ref, = refs[:n_in]
        section = lambda s: p_ref[0, :, s * width:(s + 1) * width]
    xc = section(0)
    gb = section(1)
    u = section(2) * section(3)
    xbuf_ref[HIST_PAD:HIST_PAD + tt, :] = xc
    ubuf_ref[HIST_PAD:HIST_PAD + tt, :] = u

    seen = pos0 + t_idx * tt + lax.broadcasted_iota(jnp.int32, (tt, 1), 0) + 1
    for grp, win in enumerate(POOL_WINDOWS):
        cs = slice(grp * POOL_GROUP, (grp + 1) * POOL_GROUP)
        s = xc[:, cs]
        for back in range(1, win):
            s = s + xbuf_ref[HIST_PAD - back:HIST_PAD - back + tt, cs]
        count = jnp.minimum(seen, win).astype(F32)
        pooled = s / count - xc[:, cs]
        mixed = _dot(pooled.astype(BF16), cwg_ref[grp]) * cscale_ref[:, cs]
        c_ref[0, :, cs] = mixed.astype(BF16)

    conv = (ubuf_ref[HIST_PAD - 2:HIST_PAD - 2 + tt, :] * convw_ref[0:1, :]
            + ubuf_ref[HIST_PAD - 1:HIST_PAD - 1 + tt, :] * convw_ref[1:2, :]
            + u * convw_ref[2:3, :])
    d_ref[0] = (gb * conv).astype(BF16)

    pool_tail = xbuf_ref[HIST_PAD + tt - POOL_HIST:HIST_PAD + tt, :]
    conv_tail = ubuf_ref[HIST_PAD + tt - CONV_HIST:HIST_PAD + tt, :]
    pool_out_ref[0] = pool_tail
    conv_out_ref[0] = conv_tail
    xbuf_ref[HIST_PAD - POOL_HIST:HIST_PAD, :] = pool_tail
    ubuf_ref[HIST_PAD - CONV_HIST:HIST_PAD, :] = conv_tail


def _odd_mix(rows_in, pool_prev, conv_prev, cwg, cscale, convw, *, tt, pos0):
    batch, t, _ = rows_in[0].shape
    width = pool_prev.shape[-1]
    seq_blk = lambda b, i: (b, i, 0)
    per_b = lambda b, i: (b, 0, 0)
    const = lambda b, i: (0, 0)
    project = len(rows_in) == 3
    if project:
        x, g, w = rows_in
        d = x.shape[-1]
        rows_in = (x, g.reshape(1, d), w)
        rows_specs = [pl.BlockSpec((1, tt, d), seq_blk), pl.BlockSpec((1, d), const),
                      pl.BlockSpec(w.shape, const, pipeline_mode=pl.Buffered(1))]
    else:
        rows_specs = [pl.BlockSpec((1, tt, 4 * width), seq_blk)]
    return pl.pallas_call(
        functools.partial(_odd_mix_kernel, tt=tt, pos0=pos0, project=project),
        grid=(batch, t // tt),
        in_specs=rows_specs + [
            pl.BlockSpec((1, POOL_HIST, width), per_b),
            pl.BlockSpec((1, CONV_HIST, width), per_b),
            pl.BlockSpec(cwg.shape, lambda b, i: (0, 0, 0)),
            pl.BlockSpec((1, width), lambda b, i: (0, 0)),
            pl.BlockSpec(convw.shape, lambda b, i: (0, 0)),
        ],
        out_specs=[
            pl.BlockSpec((1, tt, width), seq_blk),
            pl.BlockSpec((1, tt, width), seq_blk),
            pl.BlockSpec((1, POOL_HIST, width), per_b),
            pl.BlockSpec((1, CONV_HIST, width), per_b),
        ],
        out_shape=[
            jax.ShapeDtypeStruct((batch, t, width), BF16),
            jax.ShapeDtypeStruct((batch, t, width), BF16),
            jax.ShapeDtypeStruct((batch, POOL_HIST, width), F32),
            jax.ShapeDtypeStruct((batch, CONV_HIST, width), F32),
        ],
        scratch_shapes=[pltpu.VMEM((HIST_PAD + tt, width), F32),
                        pltpu.VMEM((HIST_PAD + tt, width), F32)],
        compiler_params=_params(("parallel", "arbitrary")),
        name="odd_mix",
    )(*rows_in, pool_prev, conv_prev, cwg, cscale.reshape(1, width), convw)


def _pick_tile(m, target):
    return target if m % target == 0 else m


def kernel(x_prompt, x_sample, cache_k, cache_v, state_pool, state_conv, page_table, norm_g, ffn_w_gate, ffn_w_up, ffn_w_down, w_in_even, w_out_even, q_norm_g, k_norm_g, sb_bias, a_v_norm_g, a_ws, a_bs, w_in_odd, w_out_odd, c_wg, c_scale, d_conv_w):
    bp, tp, d = x_prompt.shape
    bs, ts, _ = x_sample.shape
    depth = norm_g.shape[0]
    mp, ms = bp * tp, bs * ts
    sec = N_HEADS * HEAD_DIM

    tm_p = _pick_tile(mp, ROW_TILE)
    n_out = w_out_even.shape[-1]

    def ffn_pair(yp, ys, layer, half):
        return _ffn(yp, ys, norm_g[layer, 2 * half], ffn_w_gate, ffn_w_up, ffn_w_down,
                    layer=layer, half=half, tm=_pick_tile(mp, FFN_ROWS),
                    tf=_pick_tile(ffn_w_gate.shape[-1], FFN_COLS))

    yp = x_prompt.reshape(mp, d)
    ys = x_sample.reshape(ms, d)
    outs = {name: [] for name in ("kp", "vp", "ks", "vs", "av", "pp", "ps", "cp", "cs")}

    for layer in range(depth):
        i = layer // 2
        yp, ys = ffn_pair(yp, ys, layer, 0)
        if layer % 2 == 0:
            w_in_b, w_out = w_in_even[i].astype(BF16), w_out_even[i]
            tril_p = jnp.tril(a_ws[i]).astype(BF16)
            bias_p = jnp.repeat(a_bs[i].T, HEAD_DIM, axis=1)
            tril_s = jnp.tril(a_ws[i][:, :ts, :ts])
            eye = jnp.eye(bs, dtype=F32)
            wmix_s = (eye[None, :, None, :, None] * tril_s[:, None, :, None, :]).reshape(
                N_HEADS, ms, ms).astype(BF16)
            wmix_s = jnp.pad(wmix_s, ((0, 0), (0, 0), (0, max(0, HEAD_DIM - ms))))
            bias_s = jnp.tile(jnp.repeat(a_bs[i][:, :ts].T, HEAD_DIM, axis=1), (bs, 1))

            q, k, v, gated = _even_in(yp, norm_g[layer, 1], w_in_b, q_norm_g[i],
                                      k_norm_g[i], a_v_norm_g[i], tril_p, bias_p,
                                      tm=tm_p, want_avn=False)
            sb = _sb_prompt(q, k, v, sb_bias[i], batch=bp, heads=HEADS_PER_STEP)
            yp = _out_proj(gated, sb, w_out, yp, tm=tm_p, tn=n_out)
            outs["kp"].append(k.reshape(bp, tp, N_HEADS, HEAD_DIM))
            outs["vp"].append(v.reshape(bp, tp, N_HEADS, HEAD_DIM))

            q, k, v, gated, avn = _even_in(ys, norm_g[layer, 1], w_in_b, q_norm_g[i],
                                           k_norm_g[i], a_v_norm_g[i], wmix_s, bias_s, tm=ms,
                                           want_avn=True)
            n_phys = cache_k.shape[1]
            pool_shape = (cache_k.shape[0] * n_phys, PAGE * N_HEADS, HEAD_DIM)
            sb = _sb_sample(q, k, v, cache_k.reshape(pool_shape), cache_v.reshape(pool_shape),
                            page_table, sb_bias[i], batch=bs, page_base=i * n_phys,
                            pages=PAGES_PER_STEP)
            ys = _out_proj(gated, sb, w_out, ys, tm=ms, tn=n_out)
            outs["ks"].append(k.reshape(bs, ts, N_HEADS, HEAD_DIM))
            outs["vs"].append(v.reshape(bs, ts, N_HEADS, HEAD_DIM))
            outs["av"].append(avn.reshape(bs, ts, sec))
        else:
            width = c_scale.shape[-1]
            w_in_b, w_out = w_in_odd[i].astype(BF16), w_out_odd[i]
            cwg_b = c_wg[i].astype(BF16)
            c_out, d_out, pool_st, conv_st = _odd_mix(
                (yp.reshape(bp, tp, d), norm_g[layer, 1], w_in_b),
                jnp.zeros((bp, POOL_HIST, width), F32), jnp.zeros((bp, CONV_HIST, width), F32),
                cwg_b, c_scale[i], d_conv_w[i], tt=_pick_tile(tp, ROW_TILE), pos0=0)
            yp = _out_proj(c_out.reshape(mp, width), d_out.reshape(mp, width), w_out,
                           yp, tm=tm_p, tn=n_out)
            outs["pp"].append(pool_st)
            outs["cp"].append(conv_st)

            p = _norm_matmul(ys, norm_g[layer, 1], w_in_b, tm=ms, tn=width)
            past = page_table.shape[1] * PAGE
            c_out, d_out, pool_st, conv_st = _odd_mix(
                (p.reshape(bs, ts, 4 * width),), state_pool[i], state_conv[i],
                cwg_b, c_scale[i], d_conv_w[i], tt=ts, pos0=past)
            ys = _out_proj(c_out.reshape(ms, width), d_out.reshape(ms, width), w_out,
                           ys, tm=ms, tn=n_out)
            outs["ps"].append(pool_st)
            outs["cs"].append(conv_st)
        yp, ys = ffn_pair(yp, ys, layer, 1)

    return (yp.reshape(bp, tp, d), ys.reshape(bs, ts, d),
            jnp.stack(outs["kp"]), jnp.stack(outs["vp"]),
            jnp.stack(outs["ks"]), jnp.stack(outs["vs"]), jnp.stack(outs["av"]),
            jnp.stack(outs["pp"]), jnp.stack(outs["ps"]),
            jnp.stack(outs["cp"]), jnp.stack(outs["cs"]))
```
